```python
import math
import jax, jax.numpy as jnp
from jax import lax
import numpy as np

D_MODEL = 1024
BATCH = 16
SEQ = 4096
DEPTH = 2
DEC_BATCH = 32
DEC_SEQ = 32
PAST_LEN = 2048

CHUNK = 64
N_A_LAYERS = DEPTH // 2
N_B_LAYERS = DEPTH - N_A_LAYERS
CONV_CH = D_MODEL
CONV_WIDTH = 31
N_HEADS = 16
QK_NOPE = 64
QK_ROPE = 32
V_HEAD = 64
KV_LORA = 256
Q_LORA = 512
ROPE_THETA = 10000.0
QBLOCK = 128
NORM_EPS = 1e-6
ATTN_SCALE = 1.0 / math.sqrt(QK_NOPE + QK_ROPE)

kernel_name = 'yoco_conformer_conv_mla_stream_step'


def rmsnorm(x, g):
    xf = x.astype(jnp.float32)
    y = xf * lax.rsqrt(jnp.mean(xf * xf, axis=-1, keepdims=True) + NORM_EPS)
    return (y * g.astype(jnp.float32)).astype(x.dtype)


def layernorm(x, g, b):
    xf = x.astype(jnp.float32)
    mu = jnp.mean(xf, axis=-1, keepdims=True)
    xc = xf - mu
    y = xc * lax.rsqrt(jnp.mean(xc * xc, axis=-1, keepdims=True) + NORM_EPS)
    return (y * g.astype(jnp.float32) + b.astype(jnp.float32)).astype(x.dtype)


def rope(x, pos):
    half = x.shape[-1] // 2
    inv = ROPE_THETA ** (-jnp.arange(half, dtype=jnp.float32) / half)
    ang = pos.astype(jnp.float32)[:, None] * inv[None, :]
    shape = (1, pos.shape[0]) + (1,) * (x.ndim - 3) + (half,)
    cos = jnp.cos(ang).reshape(shape)
    sin = jnp.sin(ang).reshape(shape)
    xf = x.astype(jnp.float32)
    x1, x2 = xf[..., :half], xf[..., half:]
    return jnp.concatenate([x1 * cos - x2 * sin, x2 * cos + x1 * sin], axis=-1).astype(x.dtype)


def conv_module(u, buf, w_in, w_dw, b_dw, g_ln, b_ln, w_out):
    a, gl, z = jnp.split(u @ w_in, 3, axis=-1)
    v = a * jax.nn.sigmoid(gl)
    vp = jnp.concatenate([buf.astype(v.dtype), v], axis=1)
    y = lax.conv_general_dilated(vp, w_dw[:, None, :].astype(vp.dtype), window_strides=(1,),
                                 padding='VALID', dimension_numbers=('NWC', 'WIO', 'NWC'),
                                 feature_group_count=CONV_CH) + b_dw
    y = jax.nn.silu(layernorm(y, g_ln, b_ln))
    out = (y * jax.nn.silu(z)) @ w_out
    return out, vp[:, vp.shape[1] - (CONV_WIDTH - 1):]


def mla_kv_side(h, pos, g_kv_in, w_kv_a, g_kv_a):
    lat = rmsnorm(h, g_kv_in) @ w_kv_a
    c = rmsnorm(lat[..., :KV_LORA], g_kv_a)
    kr = rope(lat[..., KV_LORA:], pos)
    return c, kr


def latent_attention(q_nope, q_rope, q_pos, c, kr, k_pos, w_uk, w_uv):
    q_lat = jnp.einsum('bthn,chn->bthc', q_nope, w_uk)
    s = (jnp.einsum('bthc,bsc->bhts', q_lat, c).astype(jnp.float32)
         + jnp.einsum('bthr,bsr->bhts', q_rope, kr).astype(jnp.float32)) * ATTN_SCALE
    allowed = (k_pos[None, :] // CHUNK) <= (q_pos[:, None] // CHUNK)
    s = jnp.where(allowed[None, None], s, -jnp.inf)
    p = jax.nn.softmax(s, axis=-1).astype(c.dtype)
    o_lat = jnp.einsum('bhts,bsc->bthc', p, c)
    return jnp.einsum('bthc,chv->bthv', o_lat, w_uv)


def mla_mixer(u, pos, c, kr, k_pos, blocked, w_b_in, g_q_a, w_q_b, w_kv_b, w_b_out):
    B, T, _ = u.shape
    p = u @ w_b_in
    qa, z = p[..., :Q_LORA], p[..., Q_LORA:]
    q = (rmsnorm(qa, g_q_a) @ w_q_b).reshape(B, T, N_HEADS, QK_NOPE + QK_ROPE)
    q_nope = q[..., :QK_NOPE]
    q_rope = rope(q[..., QK_NOPE:], pos)
    w_kvb = w_kv_b.reshape(KV_LORA, N_HEADS, QK_NOPE + V_HEAD)
    w_uk, w_uv = w_kvb[..., :QK_NOPE], w_kvb[..., QK_NOPE:]
    if blocked:
        nb = T // QBLOCK
        split = lambda a: a.reshape((B, nb, QBLOCK) + a.shape[2:]).swapaxes(0, 1)
        def blk(args):
            qn, qr, qp = args
            return latent_attention(qn, qr, qp, c, kr, k_pos, w_uk, w_uv)
        o = lax.map(blk, (split(q_nope), split(q_rope), pos.reshape(nb, QBLOCK)))
        o = o.swapaxes(0, 1).reshape(B, T, N_HEADS, V_HEAD)
    else:
        o = latent_attention(q_nope, q_rope, pos, c, kr, k_pos, w_uk, w_uv)
    o = o.reshape(B, T, N_HEADS * V_HEAD)
    return (o * jax.nn.silu(z)) @ w_b_out


def trunk(x, pos, conv_bufs, past_c, past_kr, blocked,
          g_pre, g_post, w_a_in, w_a_dw, b_a_dw, g_a_ln, b_a_ln, w_a_out,
          g_kv_in, w_kv_a, g_kv_a, w_kv_b, w_b_in, g_q_a, w_q_b, w_b_out):
    h = x
    new_conv = []
    c_new = kr_new = None
    c_all = kr_all = k_pos = None
    for layer in range(DEPTH):
        u = rmsnorm(h, g_pre[layer])
        if layer < N_A_LAYERS:
            i = layer
            out, buf = conv_module(u, conv_bufs[i], w_a_in[i], w_a_dw[i], b_a_dw[i],
                                   g_a_ln[i], b_a_ln[i], w_a_out[i])
            new_conv.append(buf)
        else:
            if layer == N_A_LAYERS:
                c_new, kr_new = mla_kv_side(h, pos, g_kv_in, w_kv_a, g_kv_a)
                if past_c is None:
                    c_all, kr_all, k_pos = c_new, kr_new, pos
                else:
                    c_all = jnp.concatenate([past_c.astype(c_new.dtype), c_new], axis=1)
                    kr_all = jnp.concatenate([past_kr.astype(kr_new.dtype), kr_new], axis=1)
                    k_pos = jnp.arange(past_c.shape[1] + x.shape[1])
            j = layer - N_A_LAYERS
            out = mla_mixer(u, pos, c_all, kr_all, k_pos, blocked,
                            w_b_in[j], g_q_a[j], w_q_b[j], w_kv_b, w_b_out[j])
        h = h + rmsnorm(out, g_post[layer])
    return h, jnp.stack(new_conv, axis=0), c_new, kr_new


def setup_inputs(seed: int = 0) -> dict:
    key = jax.random.key(seed)
    ks = jax.random.split(key, 24)
    f32 = jnp.float32
    nrm = lambda k, shape, fan: jax.random.normal(k, shape, f32) * (fan ** -0.5)
    gain = lambda k, shape: 1.0 + 0.05 * jax.random.normal(k, shape, f32)
    small = lambda k, shape: 0.02 * jax.random.normal(k, shape, f32)
    return {
        'x_prompt': jax.random.normal(ks[0], (BATCH, SEQ, D_MODEL), f32),
        'x_sample': jax.random.normal(ks[1], (DEC_BATCH, DEC_SEQ, D_MODEL), f32),
        'state_conv': jax.random.normal(ks[2], (N_A_LAYERS, DEC_BATCH, CONV_WIDTH - 1, CONV_CH), f32),
        'cache_kv_latent': jax.random.normal(ks[3], (DEC_BATCH, PAST_LEN, KV_LORA), f32),
        'cache_k_rope': jax.random.normal(ks[4], (DEC_BATCH, PAST_LEN, QK_ROPE), f32),
        'g_pre': gain(ks[5], (DEPTH, D_MODEL)),
        'g_post': gain(ks[6], (DEPTH, D_MODEL)),
        'w_a_in': nrm(ks[7], (N_A_LAYERS, D_MODEL, 3 * CONV_CH), D_MODEL),
        'w_a_dw': nrm(ks[8], (N_A_LAYERS, CONV_WIDTH, CONV_CH), CONV_WIDTH),
        'b_a_dw': small(ks[9], (N_A_LAYERS, CONV_CH)),
        'g_a_ln': gain(ks[10], (N_A_LAYERS, CONV_CH)),
        'b_a_ln': small(ks[11], (N_A_LAYERS, CONV_CH)),
        'w_a_out': nrm(ks[12], (N_A_LAYERS, CONV_CH, D_MODEL), CONV_CH),
        'g_kv_in': gain(ks[13], (D_MODEL,)),
        'w_kv_a': nrm(ks[14], (D_MODEL, KV_LORA + QK_ROPE), D_MODEL),
        'g_kv_a': gain(ks[15], (KV_LORA,)),
        'w_kv_b': nrm(ks[16], (KV_LORA, N_HEADS * (QK_NOPE + V_HEAD)), KV_LORA),
        'w_b_in': nrm(ks[17], (N_B_LAYERS, D_MODEL, Q_LORA + N_HEADS * V_HEAD), D_MODEL),
        'g_q_a': gain(ks[18], (N_B_LAYERS, Q_LORA)),
        'w_q_b': nrm(ks[19], (N_B_LAYERS, Q_LORA, N_HEADS * (QK_NOPE + QK_ROPE)), Q_LORA),
        'w_b_out': nrm(ks[20], (N_B_LAYERS, N_HEADS * V_HEAD, D_MODEL), N_HEADS * V_HEAD),
    }


def reference(x_prompt, x_sample, state_conv, cache_kv_latent, cache_k_rope,
              g_pre, g_post, w_a_in, w_a_dw, b_a_dw, g_a_ln, b_a_ln, w_a_out,
              g_kv_in, w_kv_a, g_kv_a, w_kv_b, w_b_in, g_q_a, w_q_b, w_b_out):
    weights = (g_pre, g_post, w_a_in, w_a_dw, b_a_dw, g_a_ln, b_a_ln, w_a_out,
               g_kv_in, w_kv_a, g_kv_a, w_kv_b, w_b_in, g_q_a, w_q_b, w_b_out)
    pos_p = jnp.arange(x_prompt.shape[1])
    zero_bufs = jnp.zeros((N_A_LAYERS, x_prompt.shape[0], CONV_WIDTH - 1, CONV_CH), x_prompt.dtype)
    y_prompt, conv_state_prompt, kv_latent_prompt, k_rope_prompt = trunk(
        x_prompt, pos_p, zero_bufs, None, None, True, *weights)
    past_len = cache_kv_latent.shape[1]
    pos_s = past_len + jnp.arange(x_sample.shape[1])
    y_sample, conv_state_sample, kv_latent_sample, k_rope_sample = trunk(
        x_sample, pos_s, state_conv, cache_kv_latent, cache_k_rope, False, *weights)
    return (y_prompt, y_sample, conv_state_prompt, kv_latent_prompt, k_rope_prompt,
            conv_state_sample, kv_latent_sample, k_rope_sample)
```

```python
import functools
import math

import numpy as np
import jax
import jax.numpy as jnp
from jax import lax
from jax.experimental import pallas as pl
from jax.experimental.pallas import tpu as pltpu

CHUNK = 64
N_HEADS = 16
QK_NOPE = 64
QK_ROPE = 32
V_HEAD = 64
ROPE_THETA = 10000.0
NORM_EPS = 1e-6
ATTN_SCALE = 1.0 / math.sqrt(QK_NOPE + QK_ROPE)
LOG2E = 1.4426950408889634

LANES = 128
SUBLANES = 8
HEAD_PAD = 128
HIST = 32
VMEM_LIMIT = 56 * 1024 * 1024
NEG_INF = float("-inf")

_f32 = jnp.float32
_bf16 = jnp.bfloat16


def _chunk_of(pos):
    assert CHUNK & (CHUNK - 1) == 0
    return lax.shift_right_logical(pos, int(math.log2(CHUNK)))


def _rms(x, g):
    ms = jnp.mean(x * x, axis=-1, keepdims=True)
    return x * lax.rsqrt(ms + NORM_EPS) * g


def _const_spec(shape):
    nd = len(shape)
    return pl.BlockSpec(shape, lambda *_: (0,) * nd, pipeline_mode=pl.Buffered(1))


def _rope_table_kernel(inv_ref, c_ref, sa_ref, sb_ref, *, period, offset):
    rows = c_ref.shape[0]
    row = lax.broadcasted_iota(jnp.int32, (rows, LANES), 0)
    lane = lax.broadcasted_iota(jnp.int32, (rows, LANES), 1)
    if period < rows:
        assert period & (period - 1) == 0
        row = row & (period - 1)
    pos = (offset + row).astype(_f32)
    ang = pos * inv_ref[...]
    cos = jnp.cos(ang)
    sin = jnp.sin(ang)
    half = QK_ROPE // 2
    in_x1 = (lane >= QK_NOPE) & (lane < QK_NOPE + half)
    in_x2 = (lane >= QK_NOPE + half) & (lane < QK_NOPE + QK_ROPE)
    c_ref[...] = jnp.where(lane < QK_NOPE, 1.0, jnp.where(in_x1 | in_x2, cos, 0.0))
    sa_ref[...] = jnp.where(in_x1, -sin, 0.0)
    sb_ref[...] = jnp.where(in_x2, sin, 0.0)


def _rope_tables(rows, period, offset):
    half = QK_ROPE // 2
    inv = ROPE_THETA ** (-jnp.arange(half, dtype=_f32) / half)
    inv_l = jnp.tile(inv, LANES // half)[None, :]
    out = jax.ShapeDtypeStruct((rows, LANES), _f32)
    return pl.pallas_call(
        functools.partial(_rope_table_kernel, period=period, offset=offset),
        out_shape=(out, out, out),
        name="rope_tables",
    )(inv_l)


def _conv_layer_kernel(*refs, bb_n, tt, width, has_state):
    if has_state:
        (x_ref, st_ref, gpre_ref, win_ref, wdw_ref, bdw_ref, gln_ref, bln_ref, wout_ref, gpost_ref,
         h_ref, cst_ref, vbuf, ybuf) = refs
    else:
        (x_ref, gpre_ref, win_ref, wdw_ref, bdw_ref, gln_ref, bln_ref, wout_ref, gpost_ref,
         h_ref, cst_ref, vbuf, ybuf) = refs
        st_ref = None
    d = x_ref.shape[-1]
    c = wdw_ref.shape[-1]
    n_cb = c // LANES
    rows = bb_n * tt
    j = pl.program_id(1)
    pad = HIST - (width - 1)

    x = x_ref[...].reshape(rows, d)
    u = _rms(x, gpre_ref[...]).astype(_bf16)
    p = jnp.dot(u, win_ref[...], preferred_element_type=_f32)
    v = p[:, :c] * jax.nn.sigmoid(p[:, c:2 * c])
    z = p[:, 2 * c:]

    @pl.when(j == 0)
    def _():
        for bb in range(bb_n):
            for cb in range(n_cb):
                if has_state:
                    vbuf[bb, cb, 0:SUBLANES, :] = jnp.zeros((SUBLANES, LANES), _f32)
                    vbuf[bb, cb, pad:HIST, :] = st_ref[0, bb, :, cb * LANES:(cb + 1) * LANES]
                else:
                    vbuf[bb, cb, 0:HIST, :] = jnp.zeros((HIST, LANES), _f32)

    for bb in range(bb_n):
        for cb in range(n_cb):
            vbuf[bb, cb, HIST:HIST + tt, :] = v[bb * tt:(bb + 1) * tt, cb * LANES:(cb + 1) * LANES]

    rc = min(tt, 64)
    nr = rc // SUBLANES
    for cb in range(n_cb):
        lanes = pl.ds(cb * LANES, LANES)
        wb = [jnp.broadcast_to(wdw_ref[k:k + 1, lanes], (SUBLANES, LANES)) for k in range(width)]
        bias = jnp.broadcast_to(bdw_ref[0:1, lanes], (SUBLANES, LANES))
        for bb in range(bb_n):
            def chunk(i, carry, bb=bb, cb=cb, lanes=lanes, wb=wb, bias=bias):
                r0 = pl.multiple_of(i * rc, rc)
                for r in range(nr):
                    acc = bias
                    for k in range(width):
                        acc = acc + wb[k] * vbuf[bb, cb, pl.ds(r0 + SUBLANES * r + pad + k, SUBLANES, stride=1), :]
                    ybuf[pl.ds(bb * tt + r0 + SUBLANES * r, SUBLANES), lanes] = acc
                return carry
            lax.fori_loop(0, tt // rc, chunk, 0)

    for bb in range(bb_n):
        for cb in range(n_cb):
            cst_ref[0, bb, :, cb * LANES:(cb + 1) * LANES] = vbuf[bb, cb, tt + pad:tt + HIST, :]
            vbuf[bb, cb, 0:HIST, :] = vbuf[bb, cb, tt:tt + HIST, :]

    y = ybuf[...]
    mu = jnp.mean(y, axis=-1, keepdims=True)
    yc = y - mu
    var = jnp.mean(yc * yc, axis=-1, keepdims=True)
    yn = yc * lax.rsqrt(var + NORM_EPS) * gln_ref[...] + bln_ref[...]
    m = (jax.nn.silu(yn) * jax.nn.silu(z)).astype(_bf16)
    out = jnp.dot(m, wout_ref[...], preferred_element_type=_f32)
    h = x + _rms(out, gpost_ref[...])
    h_ref[...] = h.reshape(bb_n, tt, d)


def _conv_layer(x, state, g_pre, w_in, w_dw, b_dw, g_ln, b_ln, w_out, g_post, *, bb_n, tt):
    b, t, d = x.shape
    width, c = w_dw.shape
    has_state = state is not None
    assert t % tt == 0 and b % bb_n == 0 and tt % SUBLANES == 0 and tt >= HIST and c % LANES == 0
    assert width - 1 <= HIST
    grid = (b // bb_n, t // tt)
    in_specs = [pl.BlockSpec((bb_n, tt, d), lambda i, j: (i, j, 0))]
    args = [x]
    if has_state:
        in_specs.append(pl.BlockSpec((1, bb_n, width - 1, c), lambda i, j: (0, i, 0, 0)))
        args.append(state)
    in_specs += [_const_spec((1, d)), _const_spec(w_in.shape), _const_spec(w_dw.shape), _const_spec((1, c)),
                 _const_spec((1, c)), _const_spec((1, c)), _const_spec(w_out.shape), _const_spec((1, d))]
    args += [g_pre[None], w_in, w_dw, b_dw[None], g_ln[None], b_ln[None], w_out, g_post[None]]
    out_shape = (jax.ShapeDtypeStruct((b, t, d), _f32),
                 jax.ShapeDtypeStruct((1, b, width - 1, c), _f32))
    out_specs = (pl.BlockSpec((bb_n, tt, d), lambda i, j: (i, j, 0)),
                 pl.BlockSpec((1, bb_n, width - 1, c), lambda i, j: (0, i, 0, 0)))
    return pl.pallas_call(
        functools.partial(_conv_layer_kernel, bb_n=bb_n, tt=tt, width=width, has_state=has_state),
        grid=grid, in_specs=in_specs, out_specs=out_specs, out_shape=out_shape,
        scratch_shapes=[pltpu.VMEM((bb_n, c // LANES, HIST + tt, LANES), _f32),
                        pltpu.VMEM((bb_n * tt, c), _f32)],
        compiler_params=pltpu.CompilerParams(dimension_semantics=("arbitrary", "arbitrary"),
                                             vmem_limit_bytes=VMEM_LIMIT),
        name="conv_layer",
    )(*args)


def _rope_block(x, c_t, sa_t, sb_t):
    half = QK_ROPE // 2
    return x * c_t + pltpu.roll(x, LANES - half, 1) * sa_t + pltpu.roll(x, half, 1) * sb_t


def _proj_kernel(h_ref, ct_ref, sat_ref, sbt_ref, gkv_ref, wkva_ref, gkva_ref, gpre_ref, wbin_ref, gqa_ref,
                 wqb_ref, wk_ref, wv_ref, c_ref, kr_ref, q_ref, k_ref, v_ref, gate_ref, *, kv_lora, q_lora):
    h = h_ref[...]
    hn = h * lax.rsqrt(jnp.mean(h * h, axis=-1, keepdims=True) + NORM_EPS)
    c_t, sa_t, sb_t = ct_ref[...], sat_ref[...], sbt_ref[...]

    lat = jnp.dot((hn * gkv_ref[...]).astype(_bf16), wkva_ref[...], preferred_element_type=_f32)
    cl = _rms(lat[:, :kv_lora], gkva_ref[...])
    c_ref[...] = cl
    kr_blk = _rope_block(lat[:, kv_lora:kv_lora + HEAD_PAD], c_t, sa_t, sb_t)
    kr_ref[...] = kr_blk[:, QK_NOPE:QK_NOPE + QK_ROPE]
    cb = cl.astype(_bf16)
    k = jnp.dot(cb, wk_ref[...], preferred_element_type=_f32)
    for hh in range(N_HEADS):
        sl = slice(hh * HEAD_PAD, (hh + 1) * HEAD_PAD)
        k_ref[:, sl] = (k[:, sl] + kr_blk).astype(_bf16)
    v_ref[...] = jnp.dot(cb, wv_ref[...], preferred_element_type=_f32).astype(_bf16)

    p = jnp.dot((hn * gpre_ref[...]).astype(_bf16), wbin_ref[...], preferred_element_type=_f32)
    gate_ref[...] = jax.nn.silu(p[:, q_lora:]).astype(_bf16)
    qn = _rms(p[:, :q_lora], gqa_ref[...]).astype(_bf16)
    q = jnp.dot(qn, wqb_ref[...], preferred_element_type=_f32) * (ATTN_SCALE * LOG2E)
    for hh in range(N_HEADS):
        sl = slice(hh * HEAD_PAD, (hh + 1) * HEAD_PAD)
        q_ref[:, sl] = _rope_block(q[:, sl], c_t, sa_t, sb_t).astype(_bf16)


def _projections(h2, tables, n_table_blocks, g_kv_in, w_kva, g_kv_a, g_pre, w_b_in, g_q_a, w_qb, w_k, w_v, *, tt):
    n, d = h2.shape
    kv_lora = g_kv_a.shape[0]
    q_lora = g_q_a.shape[0]
    gate_w = w_b_in.shape[1] - q_lora
    assert n % tt == 0
    tok = lambda w: pl.BlockSpec((tt, w), lambda i: (i, 0))
    tab = pl.BlockSpec((tt, LANES), lambda i: (i % n_table_blocks, 0))
    in_specs = [tok(d), tab, tab, tab,
                _const_spec((1, d)), _const_spec(w_kva.shape), _const_spec((1, kv_lora)),
                _const_spec((1, d)), _const_spec(w_b_in.shape), _const_spec((1, q_lora)),
                _const_spec(w_qb.shape), _const_spec(w_k.shape), _const_spec(w_v.shape)]
    widths = (kv_lora, QK_ROPE, N_HEADS * HEAD_PAD, N_HEADS * HEAD_PAD, N_HEADS * V_HEAD, gate_w)
    dtypes = (_f32, _f32, _bf16, _bf16, _bf16, _bf16)
    out_shape = tuple(jax.ShapeDtypeStruct((n, w), dt) for w, dt in zip(widths, dtypes))
    out_specs = tuple(tok(w) for w in widths)
    return pl.pallas_call(
        functools.partial(_proj_kernel, kv_lora=kv_lora, q_lora=q_lora),
        grid=(n // tt,), in_specs=in_specs, out_specs=out_specs, out_shape=out_shape,
        compiler_params=pltpu.CompilerParams(dimension_semantics=("arbitrary",), vmem_limit_bytes=VMEM_LIMIT),
        name="projections",
    )(h2, *tables, g_kv_in[None], w_kva, g_kv_a[None], g_pre[None], w_b_in, g_q_a[None], w_qb, w_k, w_v)


def _attn_prompt_kernel(q_ref, k_ref, v_ref, o_ref, *, tq):
    t = q_ref.shape[1]
    nq = t // tq
    lane = lax.broadcasted_iota(jnp.int32, (tq, 2 * V_HEAD), 1)
    first = lane < V_HEAD
    diag_ok = (_chunk_of(lax.broadcasted_iota(jnp.int32, (tq, tq), 1))
               <= _chunk_of(lax.broadcasted_iota(jnp.int32, (tq, tq), 0)))
    nt = (((1,), (1,)), ((), ()))

    def q_tile(i, carry):
        q0 = pl.multiple_of(i * tq, tq)
        q = q_ref[0, pl.ds(q0, tq), :]
        qs = [q[:, hh * HEAD_PAD:(hh + 1) * HEAD_PAD] for hh in range(2)]

        def step(k0, state, masked):
            ms, ls, acc = state
            kt = k_ref[0, pl.ds(k0, tq), :]
            vt = v_ref[0, pl.ds(k0, tq), :]
            new_m, new_l, pv, alpha = [], [], [], []
            for hh in range(2):
                s = lax.dot_general(qs[hh], kt[:, hh * HEAD_PAD:(hh + 1) * HEAD_PAD], nt,
                                    preferred_element_type=_f32)
                if masked:
                    s = jnp.where(diag_ok, s, NEG_INF)
                m_new = jnp.maximum(ms[hh], jnp.max(s, axis=1, keepdims=True))
                a = jnp.exp2(ms[hh] - m_new)
                p = jnp.exp2(s - m_new)
                new_l.append(a * ls[hh] + jnp.sum(p, axis=1, keepdims=True))
                new_m.append(m_new)
                alpha.append(a)
                pv.append(jnp.dot(p.astype(_bf16), vt, preferred_element_type=_f32))
            acc = acc * jnp.where(first, alpha[0], alpha[1]) + jnp.where(first, pv[0], pv[1])
            return (tuple(new_m), tuple(new_l), acc)

        init = ((jnp.full((tq, 1), NEG_INF, _f32),) * 2, (jnp.zeros((tq, 1), _f32),) * 2,
                jnp.zeros((tq, 2 * V_HEAD), _f32))
        state = step(q0, init, True)
        state = lax.fori_loop(0, i, lambda jj, st: step(pl.multiple_of(jj * tq, tq), st, False), state)
        _, ls, acc = state
        o_ref[0, pl.ds(q0, tq), :] = (acc / jnp.where(first, ls[0], ls[1])).astype(o_ref.dtype)
        return carry

    lax.fori_loop(0, nq, q_tile, 0)


def _attention_prompt(q, k, v, *, tq):
    b, t, _ = q.shape
    assert t % tq == 0 and tq % CHUNK == 0
    n_pairs = N_HEADS // 2
    return pl.pallas_call(
        functools.partial(_attn_prompt_kernel, tq=tq),
        grid=(b, n_pairs),
        in_specs=[pl.BlockSpec((1, t, 2 * HEAD_PAD), lambda i, j: (i, 0, j)),
                  pl.BlockSpec((1, t, 2 * HEAD_PAD), lambda i, j: (i, 0, j)),
                  pl.BlockSpec((1, t, 2 * V_HEAD), lambda i, j: (i, 0, j))],
        out_specs=pl.BlockSpec((1, t, 2 * V_HEAD), lambda i, j: (i, 0, j)),
        out_shape=jax.ShapeDtypeStruct((b, t, N_HEADS * V_HEAD), _bf16),
        compiler_params=pltpu.CompilerParams(dimension_semantics=("arbitrary", "arbitrary"),
                                             vmem_limit_bytes=VMEM_LIMIT),
        name="attention_prompt",
    )(q, k, v)


def _attn_sample_kernel(q_ref, cc_ref, ckr_ref, cn_ref, krn_ref, wuk_ref, wuv_ref, o_ref, *, past_len, mask_new):
    ts = q_ref.shape[1]
    q = q_ref[0]
    q_lat, q_rope = [], []
    for hh in range(N_HEADS):
        qh = q[:, hh * HEAD_PAD:(hh + 1) * HEAD_PAD]
        q_lat.append(jnp.dot(qh[:, :QK_NOPE], wuk_ref[hh], preferred_element_type=_f32).astype(_bf16))
        q_rope.append(qh[:, QK_NOPE:QK_NOPE + QK_ROPE])
    q_lat = jnp.concatenate(q_lat, axis=0)
    q_rope = jnp.concatenate(q_rope, axis=0)
    nt = (((1,), (1,)), ((), ()))
    cc = cc_ref[0].astype(_bf16)
    cn = cn_ref[0].astype(_bf16)
    s_c = (lax.dot_general(q_lat, cc, nt, preferred_element_type=_f32)
           + lax.dot_general(q_rope, ckr_ref[0].astype(_bf16), nt, preferred_element_type=_f32))
    s_n = (lax.dot_general(q_lat, cn, nt, preferred_element_type=_f32)
           + lax.dot_general(q_rope, krn_ref[0].astype(_bf16), nt, preferred_element_type=_f32))
    if mask_new:
        assert ts & (ts - 1) == 0
        rows = lax.broadcasted_iota(jnp.int32, s_n.shape, 0) & (ts - 1)
        cols = lax.broadcasted_iota(jnp.int32, s_n.shape, 1)
        s_n = jnp.where(_chunk_of(past_len + cols) <= _chunk_of(past_len + rows), s_n, NEG_INF)
    m = jnp.maximum(jnp.max(s_c, axis=1, keepdims=True), jnp.max(s_n, axis=1, keepdims=True))
    p_c = jnp.exp2(s_c - m)
    p_n = jnp.exp2(s_n - m)
    l = jnp.sum(p_c, axis=1, keepdims=True) + jnp.sum(p_n, axis=1, keepdims=True)
    o_lat = (jnp.dot(p_c.astype(_bf16), cc, preferred_element_type=_f32)
             + jnp.dot(p_n.astype(_bf16), cn, preferred_element_type=_f32)) / l
    o_lat = o_lat.astype(_bf16)
    outs = [jnp.dot(o_lat[hh * ts:(hh + 1) * ts], wuv_ref[hh], preferred_element_type=_f32)
            for hh in range(N_HEADS)]
    o_ref[0] = jnp.concatenate(outs, axis=1).astype(o_ref.dtype)


def _attention_sample(q, cache_c, cache_kr, c_new, kr_new, w_uk_t, w_uv):
    b, ts, _ = q.shape
    past = cache_c.shape[1]
    kv_lora = cache_c.shape[2]
    mask_new = (past // CHUNK) != ((past + ts - 1) // CHUNK)
    blk = lambda s, w: pl.BlockSpec((1, s, w), lambda i: (i, 0, 0))
    return pl.pallas_call(
        functools.partial(_attn_sample_kernel, past_len=past, mask_new=mask_new),
        grid=(b,),
        in_specs=[blk(ts, N_HEADS * HEAD_PAD), blk(past, kv_lora), blk(past, QK_ROPE), blk(ts, kv_lora),
                  blk(ts, QK_ROPE), _const_spec(w_uk_t.shape), _const_spec(w_uv.shape)],
        out_specs=blk(ts, N_HEADS * V_HEAD),
        out_shape=jax.ShapeDtypeStruct((b, ts, N_HEADS * V_HEAD), _bf16),
        compiler_params=pltpu.CompilerParams(dimension_semantics=("arbitrary",), vmem_limit_bytes=VMEM_LIMIT),
        name="attention_sample",
    )(q, cache_c, cache_kr, c_new, kr_new, w_uk_t, w_uv)


def _out_kernel(o_ref, gate_ref, h_ref, w_ref, g_ref, y_ref):
    m = (o_ref[...].astype(_f32) * gate_ref[...].astype(_f32)).astype(_bf16)
    out = jnp.dot(m, w_ref[...], preferred_element_type=_f32)
    y_ref[...] = h_ref[...] + _rms(out, g_ref[...])


def _out_proj(o2, gate2, h2, w_out, g_post, *, tt):
    n, d = h2.shape
    w = o2.shape[1]
    assert n % tt == 0
    return pl.pallas_call(
        _out_kernel,
        grid=(n // tt,),
        in_specs=[pl.BlockSpec((tt, w), lambda i: (i, 0)), pl.BlockSpec((tt, w), lambda i: (i, 0)),
                  pl.BlockSpec((tt, d), lambda i: (i, 0)), _const_spec(w_out.shape), _const_spec((1, d))],
        out_specs=pl.BlockSpec((tt, d), lambda i: (i, 0)),
        out_shape=jax.ShapeDtypeStruct((n, d), _f32),
        compiler_params=pltpu.CompilerParams(dimension_semantics=("arbitrary",), vmem_limit_bytes=VMEM_LIMIT),
        name="out_proj",
    )(o2, gate2, h2, w_out, g_post[None])


def _prep_weights(w_a_in, w_a_out, w_kv_a, w_kv_b, w_b_in, w_q_b, w_b_out):
    kv_lora = w_kv_b.shape[0]
    pad_tail = HEAD_PAD - QK_NOPE - QK_ROPE
    wkva = jnp.concatenate([w_kv_a[:, :kv_lora], jnp.zeros((w_kv_a.shape[0], QK_NOPE), _f32),
                            w_kv_a[:, kv_lora:], jnp.zeros((w_kv_a.shape[0], pad_tail), _f32)], axis=1)
    q_lora = w_q_b.shape[0]
    wqb = w_q_b.reshape(q_lora, N_HEADS, QK_NOPE + QK_ROPE)
    wqb = jnp.pad(wqb, ((0, 0), (0, 0), (0, pad_tail))).reshape(q_lora, N_HEADS * HEAD_PAD)
    wkvb = w_kv_b.reshape(kv_lora, N_HEADS, QK_NOPE + V_HEAD)
    w_uk, w_uv = wkvb[..., :QK_NOPE], wkvb[..., QK_NOPE:]
    w_k = jnp.pad(w_uk, ((0, 0), (0, 0), (0, HEAD_PAD - QK_NOPE))).reshape(kv_lora, N_HEADS * HEAD_PAD)
    w_v = w_uv.reshape(kv_lora, N_HEADS * V_HEAD)
    bf = lambda a: a.astype(_bf16)
    return dict(w_in=bf(w_a_in), w_out_a=bf(w_a_out), wkva=bf(wkva), w_b_in=bf(w_b_in), wqb=bf(wqb),
                w_k=bf(w_k), w_v=bf(w_v), w_uk_t=bf(jnp.transpose(w_uk, (1, 2, 0))),
                w_uv_h=bf(jnp.transpose(w_uv, (1, 0, 2))), w_out_b=bf(w_b_out))


def _trunk(x, state, past, pw, gains, *, bb_n, tt, tok_tile, tq):
    (g_pre, g_post, w_dw, b_dw, g_ln, b_ln, g_kv_in, g_kv_a, g_q_a) = gains
    b, t, d = x.shape
    offset = 0 if past is None else past[0].shape[1]
    h, conv_state = _conv_layer(x, state, g_pre[0], pw["w_in"], w_dw, b_dw, g_ln, b_ln, pw["w_out_a"], g_post[0],
                                bb_n=bb_n, tt=tt)
    h2 = h.reshape(b * t, d)
    if t >= tok_tile:
        assert t % tok_tile == 0
        tables = _rope_tables(t, t, offset)
        n_tab = t // tok_tile
    else:
        assert tok_tile % t == 0
        tables = _rope_tables(tok_tile, t, offset)
        n_tab = 1
    c, kr, q, k, v, gate = _projections(h2, tables, n_tab, g_kv_in, pw["wkva"], g_kv_a, g_pre[1], pw["w_b_in"],
                                        g_q_a, pw["wqb"], pw["w_k"], pw["w_v"], tt=tok_tile)
    c3 = c.reshape(b, t, -1)
    kr3 = kr.reshape(b, t, -1)
    q3 = q.reshape(b, t, -1)
    if past is None:
        o = _attention_prompt(q3, k.reshape(b, t, -1), v.reshape(b, t, -1), tq=tq)
    else:
        o = _attention_sample(q3, past[0], past[1], c3, kr3, pw["w_uk_t"], pw["w_uv_h"])
    y = _out_proj(o.reshape(b * t, -1), gate, h2, pw["w_out_b"], g_post[1], tt=tok_tile)
    return y.reshape(b, t, d), conv_state, c3, kr3


def kernel(x_prompt, x_sample, state_conv, cache_kv_latent, cache_k_rope, g_pre, g_post, w_a_in, w_a_dw, b_a_dw, g_a_ln, b_a_ln, w_a_out, g_kv_in, w_kv_a, g_kv_a, w_kv_b, w_b_in, g_q_a, w_q_b, w_b_out):
    assert w_a_in.shape[0] == 1 and w_b_in.shape[0] == 1, "one conv layer followed by one attention layer"
    pw = _prep_weights(w_a_in[0], w_a_out[0], w_kv_a, w_kv_b, w_b_in[0], w_q_b[0], w_b_out[0])
    gains = (g_pre, g_post, w_a_dw[0], b_a_dw[0], g_a_ln[0], b_a_ln[0], g_kv_in, g_kv_a, g_q_a[0])
    y_p, cs_p, c_p, kr_p = _trunk(x_prompt, None, None, pw, gains, bb_n=1, tt=256, tok_tile=256, tq=256)
    y_s, cs_s, c_s, kr_s = _trunk(x_sample, state_conv, (cache_kv_latent, cache_k_rope), pw, gains,
                                  bb_n=8, tt=x_sample.shape[1], tok_tile=256, tq=256)
    return (y_p, y_s, cs_p, c_p, kr_p, cs_s, c_s, kr_s)
```

```python
import functools
import math

import jax
import jax.numpy as jnp
from jax import lax
from jax.experimental import pallas as pl
from jax.experimental.pallas import tpu as pltpu

CHUNK = 64
N_HEADS = 16
QK_NOPE = 64
QK_ROPE = 32
V_HEAD = 64
ROPE_THETA = 10000.0
NORM_EPS = 1e-6
ATTN_SCALE = 1.0 / math.sqrt(QK_NOPE + QK_ROPE)
LOG2E = 1.4426950408889634
Q_SCALE = ATTN_SCALE * LOG2E

LANES = 128
SUBLANES = 8
HEAD_PAD = 128
MASK_AT = QK_NOPE + QK_ROPE
HIST = 32
VMEM_LIMIT = 56 * 1024 * 1024
NEG_INF = float("-inf")
MASK_NEG = -1e30

_f32 = jnp.float32
_bf16 = jnp.bfloat16
_NT = (((1,), (1,)), ((), ()))


def _chunk_of(pos):
    assert CHUNK & (CHUNK - 1) == 0
    return lax.shift_right_logical(pos, int(math.log2(CHUNK)))


def _rms(x, g):
    ms = jnp.mean(x * x, axis=-1, keepdims=True)
    return x * lax.rsqrt(ms + NORM_EPS) * g


def _const_spec(shape):
    nd = len(shape)
    return pl.BlockSpec(shape, lambda *_: (0,) * nd, pipeline_mode=pl.Buffered(1))


def _rope_table_kernel(*refs, period, offset, n_mask):
    if n_mask:
        inv_ref, invc_ref, c_ref, sa_ref, sb_ref, e_ref, cost_ref, sint_ref = refs
    else:
        inv_ref, c_ref, sa_ref, sb_ref = refs
    rows = c_ref.shape[0]
    row = lax.broadcasted_iota(jnp.int32, (rows, LANES), 0)
    lane = lax.broadcasted_iota(jnp.int32, (rows, LANES), 1)
    if period < rows:
        assert period & (period - 1) == 0
        row = row & (period - 1)
    ang = (offset + row).astype(_f32) * inv_ref[...]
    cos = jnp.cos(ang)
    sin = jnp.sin(ang)
    half = QK_ROPE // 2
    in_x1 = (lane >= QK_NOPE) & (lane < QK_NOPE + half)
    in_x2 = (lane >= QK_NOPE + half) & (lane < QK_NOPE + QK_ROPE)
    c_ref[...] = jnp.where(lane < QK_NOPE, 1.0, jnp.where(in_x1 | in_x2, cos, 0.0))
    sa_ref[...] = jnp.where(in_x1, -sin, 0.0)
    sb_ref[...] = jnp.where(in_x2, sin, 0.0)
    if n_mask:
        assert period >= rows and n_mask & (n_mask - 1) == 0
        e_ref[...] = jnp.where(lane - MASK_AT == (_chunk_of(offset + row) & (n_mask - 1)), 1.0, 0.0)
        pos_l = (offset + lax.broadcasted_iota(jnp.int32, (half, rows), 1)).astype(_f32)
        ang_t = invc_ref[...] * pos_l
        cost_ref[...] = jnp.cos(ang_t) * Q_SCALE
        sint_ref[...] = jnp.sin(ang_t) * Q_SCALE


def _rope_tables(rows, period, offset, n_mask):
    half = QK_ROPE // 2
    inv = ROPE_THETA ** (-jnp.arange(half, dtype=_f32) / half)
    inv_l = jnp.tile(inv, LANES // half)[None, :]
    out = jax.ShapeDtypeStruct((rows, LANES), _f32)
    if n_mask:
        out_t = jax.ShapeDtypeStruct((half, rows), _f32)
        out_shape, args = (out, out, out, out, out_t, out_t), (inv_l, inv[:, None])
    else:
        out_shape, args = (out, out, out), (inv_l,)
    return pl.pallas_call(
        functools.partial(_rope_table_kernel, period=period, offset=offset, n_mask=n_mask),
        out_shape=out_shape,
        name="rope_tables",
    )(*args)


def _conv_layer_kernel(*refs, bb_n, tt, width, has_state):
    if has_state:
        (x_ref, st_ref, gpre_ref, win_ref, wdw_ref, bdw_ref, gln_ref, bln_ref, wout_ref, gpost_ref,
         h_ref, cst_ref, vbuf, ybuf) = refs
    else:
        (x_ref, gpre_ref, win_ref, wdw_ref, bdw_ref, gln_ref, bln_ref, wout_ref, gpost_ref,
         h_ref, cst_ref, vbuf, ybuf) = refs
        st_ref = None
    d = x_ref.shape[-1]
    c = wdw_ref.shape[-1]
    n_cb = c // LANES
    rows = bb_n * tt
    j = pl.program_id(1)
    pad = HIST - (width - 1)

    x = x_ref[...].reshape(rows, d)
    u = _rms(x, gpre_ref[...]).astype(_bf16)
    p = jnp.dot(u, win_ref[...], preferred_element_type=_f32)
    v = p[:, :c] * jax.nn.sigmoid(p[:, c:2 * c])
    z = p[:, 2 * c:]

    @pl.when(j == 0)
    def _():
        for bb in range(bb_n):
            for cb in range(n_cb):
                if has_state:
                    vbuf[bb, cb, 0:SUBLANES, :] = jnp.zeros((SUBLANES, LANES), _f32)
                    vbuf[bb, cb, pad:HIST, :] = st_ref[0, bb, :, cb * LANES:(cb + 1) * LANES]
                else:
                    vbuf[bb, cb, 0:HIST, :] = jnp.zeros((HIST, LANES), _f32)

    for bb in range(bb_n):
        for cb in range(n_cb):
            vbuf[bb, cb, HIST:HIST + tt, :] = v[bb * tt:(bb + 1) * tt, cb * LANES:(cb + 1) * LANES]

    rc = min(tt, 64)
    nr = rc // SUBLANES
    for cb in range(n_cb):
        lanes = pl.ds(cb * LANES, LANES)
        wb = [jnp.broadcast_to(wdw_ref[k:k + 1, lanes], (SUBLANES, LANES)) for k in range(width)]
        bias = jnp.broadcast_to(bdw_ref[0:1, lanes], (SUBLANES, LANES))
        for bb in range(bb_n):
            def chunk(i, carry, bb=bb, cb=cb, lanes=lanes, wb=wb, bias=bias):
                r0 = pl.multiple_of(i * rc, rc)
                for r in range(nr):
                    acc = bias
                    for k in range(width):
                        acc = acc + wb[k] * vbuf[bb, cb, pl.ds(r0 + SUBLANES * r + pad + k, SUBLANES, stride=1), :]
                    ybuf[pl.ds(bb * tt + r0 + SUBLANES * r, SUBLANES), lanes] = acc
                return carry
            lax.fori_loop(0, tt // rc, chunk, 0)

    for bb in range(bb_n):
        for cb in range(n_cb):
            cst_ref[0, bb, :, cb * LANES:(cb + 1) * LANES] = vbuf[bb, cb, tt + pad:tt + HIST, :]
            vbuf[bb, cb, 0:HIST, :] = vbuf[bb, cb, tt:tt + HIST, :]

    y = ybuf[...]
    mu = jnp.mean(y, axis=-1, keepdims=True)
    yc = y - mu
    var = jnp.mean(yc * yc, axis=-1, keepdims=True)
    yn = yc * lax.rsqrt(var + NORM_EPS) * gln_ref[...] + bln_ref[...]
    m = (jax.nn.silu(yn) * jax.nn.silu(z)).astype(_bf16)
    out = jnp.dot(m, wout_ref[...], preferred_element_type=_f32)
    h = x + _rms(out, gpost_ref[...])
    h_ref[...] = h.reshape(bb_n, tt, d)


def _conv_layer(x, state, g_pre, w_in, w_dw, b_dw, g_ln, b_ln, w_out, g_post, *, bb_n, tt):
    b, t, d = x.shape
    width, c = w_dw.shape
    has_state = state is not None
    assert t % tt == 0 and b % bb_n == 0 and tt % SUBLANES == 0 and tt >= HIST and c % LANES == 0
    assert width - 1 <= HIST
    grid = (b // bb_n, t // tt)
    in_specs = [pl.BlockSpec((bb_n, tt, d), lambda i, j: (i, j, 0))]
    args = [x]
    if has_state:
        in_specs.append(pl.BlockSpec((1, bb_n, width - 1, c), lambda i, j: (0, i, 0, 0)))
        args.append(state)
    in_specs += [_const_spec((1, d)), _const_spec(w_in.shape), _const_spec(w_dw.shape), _const_spec((1, c)),
                 _const_spec((1, c)), _const_spec((1, c)), _const_spec(w_out.shape), _const_spec((1, d))]
    args += [g_pre[None], w_in, w_dw, b_dw[None], g_ln[None], b_ln[None], w_out, g_post[None]]
    out_shape = (jax.ShapeDtypeStruct((b, t, d), _f32),
                 jax.ShapeDtypeStruct((1, b, width - 1, c), _f32))
    out_specs = (pl.BlockSpec((bb_n, tt, d), lambda i, j: (i, j, 0)),
                 pl.BlockSpec((1, bb_n, width - 1, c), lambda i, j: (0, i, 0, 0)))
    return pl.pallas_call(
        functools.partial(_conv_layer_kernel, bb_n=bb_n, tt=tt, width=width, has_state=has_state),
        grid=grid, in_specs=in_specs, out_specs=out_specs, out_shape=out_shape,
        scratch_shapes=[pltpu.VMEM((bb_n, c // LANES, HIST + tt, LANES), _f32),
                        pltpu.VMEM((bb_n * tt, c), _f32)],
        compiler_params=pltpu.CompilerParams(dimension_semantics=("arbitrary", "arbitrary"),
                                             vmem_limit_bytes=VMEM_LIMIT),
        name="conv_layer",
    )(*args)


def _rope_block(x, c_t, sa_t, sb_t):
    half = QK_ROPE // 2
    return x * c_t + pltpu.roll(x, LANES - half, 1) * sa_t + pltpu.roll(x, half, 1) * sb_t


def _proj_kernel(*refs, kv_lora, q_lora, feature_major):
    if feature_major:
        (h_ref, ct_ref, sat_ref, sbt_ref, et_ref, cost_ref, sint_ref, gkv_ref, wkva_ref, gkva_ref, gpre_ref,
         wbin_ref, gqa_ref, wqb_ref, wk_ref, wv_ref, c_ref, kr_ref, gate_ref, q_ref, k_ref, v_ref) = refs
    else:
        (h_ref, ct_ref, sat_ref, sbt_ref, gkv_ref, wkva_ref, gkva_ref, gpre_ref,
         wbin_ref, gqa_ref, wqb_ref, c_ref, kr_ref, gate_ref, q_ref) = refs
    h = h_ref[...]
    hn = h * lax.rsqrt(jnp.mean(h * h, axis=-1, keepdims=True) + NORM_EPS)
    c_t, sa_t, sb_t = ct_ref[...], sat_ref[...], sbt_ref[...]

    lat = jnp.dot((hn * gkv_ref[...]).astype(_bf16), wkva_ref[...], preferred_element_type=_f32)
    cl = _rms(lat[:, :kv_lora], gkva_ref[...])
    c_ref[...] = cl
    kr_blk = _rope_block(lat[:, kv_lora:kv_lora + HEAD_PAD], c_t, sa_t, sb_t)
    kr_ref[...] = kr_blk[:, QK_NOPE:QK_NOPE + QK_ROPE]

    p = jnp.dot((hn * gpre_ref[...]).astype(_bf16), wbin_ref[...], preferred_element_type=_f32)
    gate_ref[...] = jax.nn.silu(p[:, q_lora:]).astype(_bf16)
    qn = _rms(p[:, :q_lora], gqa_ref[...]).astype(_bf16)

    if not feature_major:
        q = jnp.dot(qn, wqb_ref[...], preferred_element_type=_f32) * Q_SCALE
        for hh in range(N_HEADS):
            sl = slice(hh * HEAD_PAD, (hh + 1) * HEAD_PAD)
            q_ref[:, sl] = _rope_block(q[:, sl], c_t, sa_t, sb_t).astype(_bf16)
        return

    cb = cl.astype(_bf16)
    k = jnp.dot(cb, wk_ref[...], preferred_element_type=_f32)
    kr_e = kr_blk + et_ref[...]
    for hh in range(N_HEADS):
        sl = slice(hh * HEAD_PAD, (hh + 1) * HEAD_PAD)
        k_ref[:, sl] = (k[:, sl] + kr_e).astype(_bf16)
    v_ref[0] = lax.dot_general(wv_ref[...], cb, _NT, preferred_element_type=_f32).astype(_bf16)

    q_t = lax.dot_general(wqb_ref[...], qn, _NT, preferred_element_type=_f32)
    cos_t, sin_t = cost_ref[...], sint_ref[...]
    half = QK_ROPE // 2
    tt = q_t.shape[1]
    for hh in range(N_HEADS):
        b0 = hh * HEAD_PAD
        x1 = q_t[b0 + QK_NOPE:b0 + QK_NOPE + half]
        x2 = q_t[b0 + QK_NOPE + half:b0 + MASK_AT]
        q_ref[0, b0:b0 + QK_NOPE, :] = (q_t[b0:b0 + QK_NOPE] * Q_SCALE).astype(_bf16)
        q_ref[0, b0 + QK_NOPE:b0 + QK_NOPE + half, :] = (x1 * cos_t - x2 * sin_t).astype(_bf16)
        q_ref[0, b0 + QK_NOPE + half:b0 + MASK_AT, :] = (x2 * cos_t + x1 * sin_t).astype(_bf16)
        q_ref[0, b0 + MASK_AT:b0 + HEAD_PAD, :] = jnp.zeros((HEAD_PAD - MASK_AT, tt), _bf16)


def _projections(h2, seq_len, tables, g_kv_in, w_kva, g_kv_a, g_pre, w_b_in, g_q_a, w_qb, w_k, w_v, *, tt,
                 feature_major):
    n, d = h2.shape
    kv_lora = g_kv_a.shape[0]
    q_lora = g_q_a.shape[0]
    gate_w = w_b_in.shape[1] - q_lora
    assert n % tt == 0
    nt = max(seq_len // tt, 1)
    tok = lambda w: pl.BlockSpec((tt, w), lambda i: (i, 0))
    tab = pl.BlockSpec((tt, LANES), lambda i: (i % nt, 0))
    fm = lambda w: pl.BlockSpec((1, w, tt), lambda i: (i // nt, 0, i % nt))
    half = QK_ROPE // 2
    in_specs = [tok(d), tab, tab, tab]
    if feature_major:
        tab_t = pl.BlockSpec((half, tt), lambda i: (0, i % nt))
        in_specs += [tab, tab_t, tab_t]
    in_specs += [_const_spec((1, d)), _const_spec(w_kva.shape), _const_spec((1, kv_lora)),
                 _const_spec((1, d)), _const_spec(w_b_in.shape), _const_spec((1, q_lora)), _const_spec(w_qb.shape)]
    args = [h2, *tables, g_kv_in[None], w_kva, g_kv_a[None], g_pre[None], w_b_in, g_q_a[None], w_qb]
    out_shape = [jax.ShapeDtypeStruct((n, kv_lora), _f32), jax.ShapeDtypeStruct((n, QK_ROPE), _f32),
                 jax.ShapeDtypeStruct((n, gate_w), _bf16)]
    out_specs = [tok(kv_lora), tok(QK_ROPE), tok(gate_w)]
    if feature_major:
        nb = n // seq_len
        in_specs += [_const_spec(w_k.shape), _const_spec(w_v.shape)]
        args += [w_k, w_v]
        out_shape += [jax.ShapeDtypeStruct((nb, N_HEADS * HEAD_PAD, seq_len), _bf16),
                      jax.ShapeDtypeStruct((n, N_HEADS * HEAD_PAD), _bf16),
                      jax.ShapeDtypeStruct((nb, N_HEADS * V_HEAD, seq_len), _bf16)]
        out_specs += [fm(N_HEADS * HEAD_PAD), tok(N_HEADS * HEAD_PAD), fm(N_HEADS * V_HEAD)]
    else:
        out_shape.append(jax.ShapeDtypeStruct((n, N_HEADS * HEAD_PAD), _bf16))
        out_specs.append(tok(N_HEADS * HEAD_PAD))
    return pl.pallas_call(
        functools.partial(_proj_kernel, kv_lora=kv_lora, q_lora=q_lora, feature_major=feature_major),
        grid=(n // tt,), in_specs=in_specs, out_specs=tuple(out_specs), out_shape=tuple(out_shape),
        compiler_params=pltpu.CompilerParams(dimension_semantics=("arbitrary",), vmem_limit_bytes=VMEM_LIMIT),
        name="projections",
    )(*args)


def _attn_prompt_kernel(q_ref, k_ref, v_ref, o_ref, s_buf, m_ref, l_ref, acc_ref, *, tq):
    t = q_ref.shape[2]
    nq = t // tq
    total = nq * (nq + 1) // 2
    n_mask = tq // CHUNK
    pad_rows = HEAD_PAD - MASK_AT
    r = lax.broadcasted_iota(jnp.int32, (pad_rows, tq), 0)
    qc = _chunk_of(lax.broadcasted_iota(jnp.int32, (pad_rows, tq), 1))
    mask_rows = jnp.where((r < n_mask) & (r > qc), MASK_NEG, 0.0).astype(_f32)

    m_ref[...] = jnp.zeros_like(m_ref)
    l_ref[...] = jnp.zeros_like(l_ref)
    acc_ref[...] = jnp.zeros_like(acc_ref)

    def nxt(i, j):
        wrap = j + 1 > i
        return jnp.where(wrap, i + 1, i), jnp.where(wrap, 0, j + 1)

    def prefetch(i, j, slot):
        i = jnp.minimum(i, nq - 1)
        q0 = pl.multiple_of(i * tq, tq)
        k0 = pl.multiple_of(j * tq, tq)
        msel = (mask_rows * jnp.where(i == j, 1.0, 0.0)).astype(_bf16)
        for hh in range(2):
            q_t = jnp.concatenate([q_ref[0, hh * HEAD_PAD:hh * HEAD_PAD + MASK_AT, pl.ds(q0, tq)], msel], axis=0)
            k_t = k_ref[0, pl.ds(k0, tq), hh * HEAD_PAD:(hh + 1) * HEAD_PAD]
            s_buf[slot, hh] = jnp.dot(k_t, q_t, preferred_element_type=_f32)

    def consume(i, j, slot):
        q0 = pl.multiple_of(i * tq, tq)
        k0 = pl.multiple_of(j * tq, tq)
        restart = jnp.where(j == 0, NEG_INF, 0.0)
        for hh in range(2):
            m_old = m_ref[hh] + restart
            m_new = jnp.maximum(m_old, jnp.max(s_buf[slot, hh], axis=0, keepdims=True))
            alpha = jnp.exp2(m_old - m_new)
            p = jnp.exp2(s_buf[slot, hh] - m_new)
            l_new = alpha * l_ref[hh] + jnp.sum(p.reshape(tq // SUBLANES, SUBLANES, tq), axis=0)
            v_t = v_ref[0, hh * V_HEAD:(hh + 1) * V_HEAD, pl.ds(k0, tq)]
            acc = alpha * acc_ref[hh] + jnp.dot(v_t, p.astype(_bf16), preferred_element_type=_f32)
            m_ref[hh] = m_new
            l_ref[hh] = l_new
            acc_ref[hh] = acc
            o_ref[0, hh * V_HEAD:(hh + 1) * V_HEAD, pl.ds(q0, tq)] = (
                acc / jnp.sum(l_new, axis=0, keepdims=True)).astype(o_ref.dtype)

    zero = jnp.int32(0)
    prefetch(zero, zero, 0)

    def body(u, carry):
        i, j = carry
        i1, j1 = nxt(i, j)
        i2, j2 = nxt(i1, j1)
        prefetch(i1, j1, 1)
        consume(i, j, 0)
        prefetch(i2, j2, 0)
        consume(i1, j1, 1)
        return i2, j2

    i, j = lax.fori_loop(0, total // 2, body, (zero, zero))
    if total % 2:
        consume(i, j, 0)


def _attention_prompt(q_t, k, v_t, *, tq):
    b, _, t = q_t.shape
    assert t % tq == 0 and tq % CHUNK == 0 and tq // CHUNK <= HEAD_PAD - MASK_AT
    n_pairs = N_HEADS // 2
    return pl.pallas_call(
        functools.partial(_attn_prompt_kernel, tq=tq),
        grid=(b, n_pairs),
        in_specs=[pl.BlockSpec((1, 2 * HEAD_PAD, t), lambda i, j: (i, j, 0)),
                  pl.BlockSpec((1, t, 2 * HEAD_PAD), lambda i, j: (i, 0, j)),
                  pl.BlockSpec((1, 2 * V_HEAD, t), lambda i, j: (i, j, 0))],
        out_specs=pl.BlockSpec((1, 2 * V_HEAD, t), lambda i, j: (i, j, 0)),
        out_shape=jax.ShapeDtypeStruct((b, N_HEADS * V_HEAD, t), _bf16),
        scratch_shapes=[pltpu.VMEM((2, 2, tq, tq), _f32), pltpu.VMEM((2, 1, tq), _f32),
                        pltpu.VMEM((2, SUBLANES, tq), _f32), pltpu.VMEM((2, V_HEAD, tq), _f32)],
        compiler_params=pltpu.CompilerParams(dimension_semantics=("arbitrary", "arbitrary"),
                                             vmem_limit_bytes=VMEM_LIMIT),
        name="attention_prompt",
    )(q_t, k, v_t)


def _attn_sample_kernel(q_ref, cc_ref, ckr_ref, cn_ref, krn_ref, wuk_ref, wuv_ref, o_ref, *, past_len, mask_new):
    ts = q_ref.shape[1]
    q = q_ref[0]
    q_lat, q_rope = [], []
    for hh in range(N_HEADS):
        qh = q[:, hh * HEAD_PAD:(hh + 1) * HEAD_PAD]
        q_lat.append(jnp.dot(qh[:, :QK_NOPE], wuk_ref[hh], preferred_element_type=_f32).astype(_bf16))
        q_rope.append(qh[:, QK_NOPE:QK_NOPE + QK_ROPE])
    q_lat = jnp.concatenate(q_lat, axis=0)
    q_rope = jnp.concatenate(q_rope, axis=0)
    cc = cc_ref[0].astype(_bf16)
    cn = cn_ref[0].astype(_bf16)
    s_c = (lax.dot_general(q_lat, cc, _NT, preferred_element_type=_f32)
           + lax.dot_general(q_rope, ckr_ref[0].astype(_bf16), _NT, preferred_element_type=_f32))
    s_n = (lax.dot_general(q_lat, cn, _NT, preferred_element_type=_f32)
           + lax.dot_general(q_rope, krn_ref[0].astype(_bf16), _NT, preferred_element_type=_f32))
    if mask_new:
        assert ts & (ts - 1) == 0
        rows = lax.broadcasted_iota(jnp.int32, s_n.shape, 0) & (ts - 1)
        cols = lax.broadcasted_iota(jnp.int32, s_n.shape, 1)
        s_n = jnp.where(_chunk_of(past_len + cols) <= _chunk_of(past_len + rows), s_n, NEG_INF)
    m = jnp.maximum(jnp.max(s_c, axis=1, keepdims=True), jnp.max(s_n, axis=1, keepdims=True))
    p_c = jnp.exp2(s_c - m)
    p_n = jnp.exp2(s_n - m)
    l = jnp.sum(p_c, axis=1, keepdims=True) + jnp.sum(p_n, axis=1, keepdims=True)
    o_lat = (jnp.dot(p_c.astype(_bf16), cc, preferred_element_type=_f32)
             + jnp.dot(p_n.astype(_bf16), cn, preferred_element_type=_f32)) / l
    o_lat = o_lat.astype(_bf16)
    outs = [jnp.dot(o_lat[hh * ts:(hh + 1) * ts], wuv_ref[hh], preferred_element_type=_f32)
            for hh in range(N_HEADS)]
    o_ref[0] = jnp.concatenate(outs, axis=1).astype(o_ref.dtype)


def _attention_sample(q, cache_c, cache_kr, c_new, kr_new, w_uk_t, w_uv):
    b, ts, _ = q.shape
    past = cache_c.shape[1]
    kv_lora = cache_c.shape[2]
    mask_new = (past // CHUNK) != ((past + ts - 1) // CHUNK)
    blk = lambda s, w: pl.BlockSpec((1, s, w), lambda i: (i, 0, 0))
    return pl.pallas_call(
        functools.partial(_attn_sample_kernel, past_len=past, mask_new=mask_new),
        grid=(b,),
        in_specs=[blk(ts, N_HEADS * HEAD_PAD), blk(past, kv_lora), blk(past, QK_ROPE), blk(ts, kv_lora),
                  blk(ts, QK_ROPE), _const_spec(w_uk_t.shape), _const_spec(w_uv.shape)],
        out_specs=blk(ts, N_HEADS * V_HEAD),
        out_shape=jax.ShapeDtypeStruct((b, ts, N_HEADS * V_HEAD), _bf16),
        compiler_params=pltpu.CompilerParams(dimension_semantics=("arbitrary",), vmem_limit_bytes=VMEM_LIMIT),
        name="attention_sample",
    )(q, cache_c, cache_kr, c_new, kr_new, w_uk_t, w_uv)


def _out_kernel(o_ref, gate_ref, h_ref, w_ref, g_ref, y_ref, *, feature_major):
    if feature_major:
        o = o_ref[0].astype(_f32).T
    else:
        o = o_ref[...].astype(_f32)
    m = (o * gate_ref[...].astype(_f32)).astype(_bf16)
    out = jnp.dot(m, w_ref[...], preferred_element_type=_f32)
    y_ref[...] = h_ref[...] + _rms(out, g_ref[...])


def _out_proj(o, gate2, h2, w_out, g_post, *, tt, feature_major):
    n, d = h2.shape
    w = gate2.shape[1]
    assert n % tt == 0
    if feature_major:
        nt = o.shape[2] // tt
        o_spec = pl.BlockSpec((1, w, tt), lambda i: (i // nt, 0, i % nt))
    else:
        o_spec = pl.BlockSpec((tt, w), lambda i: (i, 0))
    return pl.pallas_call(
        functools.partial(_out_kernel, feature_major=feature_major),
        grid=(n // tt,),
        in_specs=[o_spec, pl.BlockSpec((tt, w), lambda i: (i, 0)),
                  pl.BlockSpec((tt, d), lambda i: (i, 0)), _const_spec(w_out.shape), _const_spec((1, d))],
        out_specs=pl.BlockSpec((tt, d), lambda i: (i, 0)),
        out_shape=jax.ShapeDtypeStruct((n, d), _f32),
        compiler_params=pltpu.CompilerParams(dimension_semantics=("arbitrary",), vmem_limit_bytes=VMEM_LIMIT),
        name="out_proj",
    )(o, gate2, h2, w_out, g_post[None])


def _prep_weights(w_a_in, w_a_out, w_kv_a, w_kv_b, w_b_in, w_q_b, w_b_out):
    kv_lora = w_kv_b.shape[0]
    pad_tail = HEAD_PAD - MASK_AT
    wkva = jnp.concatenate([w_kv_a[:, :kv_lora], jnp.zeros((w_kv_a.shape[0], QK_NOPE), _f32),
                            w_kv_a[:, kv_lora:], jnp.zeros((w_kv_a.shape[0], pad_tail), _f32)], axis=1)
    q_lora = w_q_b.shape[0]
    wqb = w_q_b.reshape(q_lora, N_HEADS, MASK_AT)
    wqb = jnp.pad(wqb, ((0, 0), (0, 0), (0, pad_tail))).reshape(q_lora, N_HEADS * HEAD_PAD)
    wkvb = w_kv_b.reshape(kv_lora, N_HEADS, QK_NOPE + V_HEAD)
    w_uk, w_uv = wkvb[..., :QK_NOPE], wkvb[..., QK_NOPE:]
    w_k = jnp.pad(w_uk, ((0, 0), (0, 0), (0, HEAD_PAD - QK_NOPE))).reshape(kv_lora, N_HEADS * HEAD_PAD)
    w_v = w_uv.reshape(kv_lora, N_HEADS * V_HEAD)
    bf = lambda a: a.astype(_bf16)
    return dict(w_in=bf(w_a_in), w_out_a=bf(w_a_out), wkva=bf(wkva), w_b_in=bf(w_b_in), wqb=bf(wqb),
                wqb_t=bf(wqb.T), w_k=bf(w_k), w_v_t=bf(w_v.T), w_uk_t=bf(jnp.transpose(w_uk, (1, 2, 0))),
                w_uv_h=bf(jnp.transpose(w_uv, (1, 0, 2))), w_out_b=bf(w_b_out))


def _trunk(x, state, past, pw, gains, *, bb_n, tt, tok_tile, tq):
    (g_pre, g_post, w_dw, b_dw, g_ln, b_ln, g_kv_in, g_kv_a, g_q_a) = gains
    b, t, d = x.shape
    prompt = past is None
    offset = 0 if prompt else past[0].shape[1]
    h, conv_state = _conv_layer(x, state, g_pre[0], pw["w_in"], w_dw, b_dw, g_ln, b_ln, pw["w_out_a"], g_post[0],
                                bb_n=bb_n, tt=tt)
    h2 = h.reshape(b * t, d)
    if t >= tok_tile:
        assert t % tok_tile == 0
        tables = _rope_tables(t, t, offset, tq // CHUNK if prompt else 0)
    else:
        assert tok_tile % t == 0 and not prompt
        tables = _rope_tables(tok_tile, t, offset, 0)
    outs = _projections(h2, t, tables, g_kv_in, pw["wkva"], g_kv_a, g_pre[1], pw["w_b_in"], g_q_a,
                        pw["wqb_t"] if prompt else pw["wqb"], pw["w_k"], pw["w_v_t"], tt=tok_tile,
                        feature_major=prompt)
    c3 = outs[0].reshape(b, t, -1)
    kr3 = outs[1].reshape(b, t, -1)
    gate = outs[2]
    if prompt:
        q_t, k, v_t = outs[3:]
        o = _attention_prompt(q_t, k.reshape(b, t, -1), v_t, tq=tq)
    else:
        o = _attention_sample(outs[3].reshape(b, t, -1), past[0], past[1], c3, kr3, pw["w_uk_t"], pw["w_uv_h"])
        o = o.reshape(b * t, -1)
    y = _out_proj(o, gate, h2, pw["w_out_b"], g_post[1], tt=tok_tile, feature_major=prompt)
    return y.reshape(b, t, d), conv_state, c3, kr3


def kernel(x_prompt, x_sample, state_conv, cache_kv_latent, cache_k_rope, g_pre, g_post, w_a_in, w_a_dw, b_a_dw, g_a_ln, b_a_ln, w_a_out, g_kv_in, w_kv_a, g_kv_a, w_kv_b, w_b_in, g_q_a, w_q_b, w_b_out):
    assert w_a_in.shape[0] == 1 and w_b_in.shape[0] == 1, "one conv layer followed by one attention layer"
    pw = _prep_weights(w_a_in[0], w_a_out[0], w_kv_a, w_kv_b, w_b_in[0], w_q_b[0], w_b_out[0])
    gains = (g_pre, g_post, w_a_dw[0], b_a_dw[0], g_a_ln[0], b_a_ln[0], g_kv_in, g_kv_a, g_q_a[0])
    y_p, cs_p, c_p, kr_p = _trunk(x_prompt, None, None, pw, gains, bb_n=1, tt=256, tok_tile=256, tq=256)
    y_s, cs_s, c_s, kr_s = _trunk(x_sample, state_conv, (cache_kv_latent, cache_k_rope), pw, gains,
                                  bb_n=8, tt=x_sample.shape[1], tok_tile=256, tq=256)
    return (y_p, y_s, cs_p, c_p, kr_p, cs_s, c_s, kr_s)
```

```python
import functools
import math

import jax
import jax.numpy as jnp
from jax import lax
from jax.experimental import pallas as pl
from jax.experimental.pallas import tpu as pltpu

CHUNK = 64
N_HEADS = 16
QK_NOPE = 64
QK_ROPE = 32
V_HEAD = 64
ROPE_THETA = 10000.0
NORM_EPS = 1e-6
ATTN_SCALE = 1.0 / math.sqrt(QK_NOPE + QK_ROPE)
LOG2E = 1.4426950408889634
Q_SCALE = ATTN_SCALE * LOG2E

LANES = 128
SUBLANES = 8
HEAD_PAD = 128
MASK_AT = QK_NOPE + QK_ROPE
HIST = 32
VMEM_LIMIT = 56 * 1024 * 1024
NEG_INF = float("-inf")
MASK_NEG = -1e30

_f32 = jnp.float32
_bf16 = jnp.bfloat16
_NT = (((1,), (1,)), ((), ()))


def _chunk_of(pos):
    assert CHUNK & (CHUNK - 1) == 0
    return lax.shift_right_logical(pos, int(math.log2(CHUNK)))


def _rms(x, g):
    ms = jnp.mean(x * x, axis=-1, keepdims=True)
    return x * lax.rsqrt(ms + NORM_EPS) * g


def _const_spec(shape):
    nd = len(shape)
    return pl.BlockSpec(shape, lambda *_: (0,) * nd, pipeline_mode=pl.Buffered(1))


def _rope_table_kernel(*refs, period, offset, n_mask):
    if n_mask:
        inv_ref, invc_ref, c_ref, sa_ref, sb_ref, e_ref, cost_ref, sint_ref = refs
    else:
        inv_ref, c_ref, sa_ref, sb_ref = refs
    rows = c_ref.shape[0]
    row = lax.broadcasted_iota(jnp.int32, (rows, LANES), 0)
    lane = lax.broadcasted_iota(jnp.int32, (rows, LANES), 1)
    if period < rows:
        assert period & (period - 1) == 0
        row = row & (period - 1)
    ang = (offset + row).astype(_f32) * inv_ref[...]
    cos = jnp.cos(ang)
    sin = jnp.sin(ang)
    half = QK_ROPE // 2
    in_x1 = (lane >= QK_NOPE) & (lane < QK_NOPE + half)
    in_x2 = (lane >= QK_NOPE + half) & (lane < QK_NOPE + QK_ROPE)
    c_ref[...] = jnp.where(lane < QK_NOPE, 1.0, jnp.where(in_x1 | in_x2, cos, 0.0))
    sa_ref[...] = jnp.where(in_x1, -sin, 0.0)
    sb_ref[...] = jnp.where(in_x2, sin, 0.0)
    if n_mask:
        assert period >= rows and n_mask & (n_mask - 1) == 0
        e_ref[...] = jnp.where(lane - MASK_AT == (_chunk_of(offset + row) & (n_mask - 1)), 1.0, 0.0)
        pos_l = (offset + lax.broadcasted_iota(jnp.int32, (half, rows), 1)).astype(_f32)
        ang_t = invc_ref[...] * pos_l
        cost_ref[...] = jnp.cos(ang_t) * Q_SCALE
        sint_ref[...] = jnp.sin(ang_t) * Q_SCALE


def _rope_tables(rows, period, offset, n_mask):
    half = QK_ROPE // 2
    inv = ROPE_THETA ** (-jnp.arange(half, dtype=_f32) / half)
    inv_l = jnp.tile(inv, LANES // half)[None, :]
    out = jax.ShapeDtypeStruct((rows, LANES), _f32)
    if n_mask:
        out_t = jax.ShapeDtypeStruct((half, rows), _f32)
        out_shape, args = (out, out, out, out, out_t, out_t), (inv_l, inv[:, None])
    else:
        out_shape, args = (out, out, out), (inv_l,)
    return pl.pallas_call(
        functools.partial(_rope_table_kernel, period=period, offset=offset, n_mask=n_mask),
        out_shape=out_shape,
        name="rope_tables",
    )(*args)


def _conv_layer_kernel(*refs, bb_n, tt, width, has_state):
    if has_state:
        (x_ref, st_ref, gpre_ref, win_ref, wdw_ref, bdw_ref, gln_ref, bln_ref, wout_ref, gpost_ref,
         h_ref, cst_ref, vbuf, ybuf) = refs
    else:
        (x_ref, gpre_ref, win_ref, wdw_ref, bdw_ref, gln_ref, bln_ref, wout_ref, gpost_ref,
         h_ref, cst_ref, vbuf, ybuf) = refs
        st_ref = None
    d = x_ref.shape[-1]
    c = wdw_ref.shape[-1]
    n_cb = c // LANES
    rows = bb_n * tt
    j = pl.program_id(1)
    pad = HIST - (width - 1)

    x = x_ref[...].reshape(rows, d)
    u = _rms(x, gpre_ref[...]).astype(_bf16)
    p = jnp.dot(u, win_ref[...], preferred_element_type=_f32)
    v = p[:, :c] * jax.nn.sigmoid(p[:, c:2 * c])
    z = p[:, 2 * c:]

    @pl.when(j == 0)
    def _():
        for bb in range(bb_n):
            for cb in range(n_cb):
                if has_state:
                    vbuf[bb, cb, 0:SUBLANES, :] = jnp.zeros((SUBLANES, LANES), _f32)
                    vbuf[bb, cb, pad:HIST, :] = st_ref[0, bb, :, cb * LANES:(cb + 1) * LANES]
                else:
                    vbuf[bb, cb, 0:HIST, :] = jnp.zeros((HIST, LANES), _f32)

    for bb in range(bb_n):
        for cb in range(n_cb):
            vbuf[bb, cb, HIST:HIST + tt, :] = v[bb * tt:(bb + 1) * tt, cb * LANES:(cb + 1) * LANES]

    rc = min(tt, 64)
    nr = rc // SUBLANES
    for cb in range(n_cb):
        lanes = pl.ds(cb * LANES, LANES)
        wb = [jnp.broadcast_to(wdw_ref[k:k + 1, lanes], (SUBLANES, LANES)) for k in range(width)]
        bias = jnp.broadcast_to(bdw_ref[0:1, lanes], (SUBLANES, LANES))
        for bb in range(bb_n):
            def chunk(i, carry, bb=bb, cb=cb, lanes=lanes, wb=wb, bias=bias):
                r0 = pl.multiple_of(i * rc, rc)
                for r in range(nr):
                    acc = bias
                    for k in range(width):
                        acc = acc + wb[k] * vbuf[bb, cb, pl.ds(r0 + SUBLANES * r + pad + k, SUBLANES, stride=1), :]
                    ybuf[pl.ds(bb * tt + r0 + SUBLANES * r, SUBLANES), lanes] = acc
                return carry
            lax.fori_loop(0, tt // rc, chunk, 0)

    for bb in range(bb_n):
        for cb in range(n_cb):
            cst_ref[0, bb, :, cb * LANES:(cb + 1) * LANES] = vbuf[bb, cb, tt + pad:tt + HIST, :]
            vbuf[bb, cb, 0:HIST, :] = vbuf[bb, cb, tt:tt + HIST, :]

    y = ybuf[...]
    mu = jnp.mean(y, axis=-1, keepdims=True)
    yc = y - mu
    var = jnp.mean(yc * yc, axis=-1, keepdims=True)
    yn = yc * lax.rsqrt(var + NORM_EPS) * gln_ref[...] + bln_ref[...]
    m = (jax.nn.silu(yn) * jax.nn.silu(z)).astype(_bf16)
    out = jnp.dot(m, wout_ref[...], preferred_element_type=_f32)
    h = x + _rms(out, gpost_ref[...])
    h_ref[...] = h.reshape(bb_n, tt, d)


def _conv_layer(x, state, g_pre, w_in, w_dw, b_dw, g_ln, b_ln, w_out, g_post, *, bb_n, tt):
    b, t, d = x.shape
    width, c = w_dw.shape
    has_state = state is not None
    assert t % tt == 0 and b % bb_n == 0 and tt % SUBLANES == 0 and tt >= HIST and c % LANES == 0
    assert width - 1 <= HIST
    grid = (b // bb_n, t // tt)
    in_specs = [pl.BlockSpec((bb_n, tt, d), lambda i, j: (i, j, 0))]
    args = [x]
    if has_state:
        in_specs.append(pl.BlockSpec((1, bb_n, width - 1, c), lambda i, j: (0, i, 0, 0)))
        args.append(state)
    in_specs += [_const_spec((1, d)), _const_spec(w_in.shape), _const_spec(w_dw.shape), _const_spec((1, c)),
                 _const_spec((1, c)), _const_spec((1, c)), _const_spec(w_out.shape), _const_spec((1, d))]
    args += [g_pre[None], w_in, w_dw, b_dw[None], g_ln[None], b_ln[None], w_out, g_post[None]]
    out_shape = (jax.ShapeDtypeStruct((b, t, d), _f32),
                 jax.ShapeDtypeStruct((1, b, width - 1, c), _f32))
    out_specs = (pl.BlockSpec((bb_n, tt, d), lambda i, j: (i, j, 0)),
                 pl.BlockSpec((1, bb_n, width - 1, c), lambda i, j: (0, i, 0, 0)))
    return pl.pallas_call(
        functools.partial(_conv_layer_kernel, bb_n=bb_n, tt=tt, width=width, has_state=has_state),
        grid=grid, in_specs=in_specs, out_specs=out_specs, out_shape=out_shape,
        scratch_shapes=[pltpu.VMEM((bb_n, c // LANES, HIST + tt, LANES), _f32),
                        pltpu.VMEM((bb_n * tt, c), _f32)],
        compiler_params=pltpu.CompilerParams(dimension_semantics=("arbitrary", "arbitrary"),
                                             vmem_limit_bytes=VMEM_LIMIT),
        name="conv_layer",
    )(*args)


def _rope_block(x, c_t, sa_t, sb_t):
    half = QK_ROPE // 2
    return x * c_t + pltpu.roll(x, LANES - half, 1) * sa_t + pltpu.roll(x, half, 1) * sb_t


def _proj_kernel(*refs, kv_lora, q_lora, feature_major):
    if feature_major:
        (h_ref, ct_ref, sat_ref, sbt_ref, et_ref, cost_ref, sint_ref, gkv_ref, wkva_ref, gkva_ref, gpre_ref,
         wbin_ref, gqa_ref, wqb_ref, wk_ref, wv_ref, c_ref, kr_ref, gate_ref, q_ref, k_ref, v_ref) = refs
    else:
        (h_ref, ct_ref, sat_ref, sbt_ref, gkv_ref, wkva_ref, gkva_ref, gpre_ref,
         wbin_ref, gqa_ref, wqb_ref, c_ref, kr_ref, gate_ref, q_ref) = refs
    h = h_ref[...]
    hn = h * lax.rsqrt(jnp.mean(h * h, axis=-1, keepdims=True) + NORM_EPS)
    c_t, sa_t, sb_t = ct_ref[...], sat_ref[...], sbt_ref[...]

    lat = jnp.dot((hn * gkv_ref[...]).astype(_bf16), wkva_ref[...], preferred_element_type=_f32)
    cl = _rms(lat[:, :kv_lora], gkva_ref[...])
    c_ref[...] = cl
    kr_blk = _rope_block(lat[:, kv_lora:kv_lora + HEAD_PAD], c_t, sa_t, sb_t)
    kr_ref[...] = kr_blk[:, QK_NOPE:QK_NOPE + QK_ROPE]

    p = jnp.dot((hn * gpre_ref[...]).astype(_bf16), wbin_ref[...], preferred_element_type=_f32)
    gate_ref[...] = jax.nn.silu(p[:, q_lora:]).astype(_bf16)
    qn = _rms(p[:, :q_lora], gqa_ref[...]).astype(_bf16)

    if not feature_major:
        q = jnp.dot(qn, wqb_ref[...], preferred_element_type=_f32) * Q_SCALE
        for hh in range(N_HEADS):
            sl = slice(hh * HEAD_PAD, (hh + 1) * HEAD_PAD)
            q_ref[:, sl] = _rope_block(q[:, sl], c_t, sa_t, sb_t).astype(_bf16)
        return

    cb = cl.astype(_bf16)
    k = jnp.dot(cb, wk_ref[...], preferred_element_type=_f32)
    kr_e = kr_blk + et_ref[...]
    for hh in range(N_HEADS):
        sl = slice(hh * HEAD_PAD, (hh + 1) * HEAD_PAD)
        k_ref[:, sl] = (k[:, sl] + kr_e).astype(_bf16)
    v_ref[0] = lax.dot_general(wv_ref[...], cb, _NT, preferred_element_type=_f32).astype(_bf16)

    q_t = lax.dot_general(wqb_ref[...], qn, _NT, preferred_element_type=_f32)
    cos_t, sin_t = cost_ref[...], sint_ref[...]
    half = QK_ROPE // 2
    tt = q_t.shape[1]
    for hh in range(N_HEADS):
        b0 = hh * HEAD_PAD
        x1 = q_t[b0 + QK_NOPE:b0 + QK_NOPE + half]
        x2 = q_t[b0 + QK_NOPE + half:b0 + MASK_AT]
        q_ref[0, b0:b0 + QK_NOPE, :] = (q_t[b0:b0 + QK_NOPE] * Q_SCALE).astype(_bf16)
        q_ref[0, b0 + QK_NOPE:b0 + QK_NOPE + half, :] = (x1 * cos_t - x2 * sin_t).astype(_bf16)
        q_ref[0, b0 + QK_NOPE + half:b0 + MASK_AT, :] = (x2 * cos_t + x1 * sin_t).astype(_bf16)
        q_ref[0, b0 + MASK_AT:b0 + HEAD_PAD, :] = jnp.zeros((HEAD_PAD - MASK_AT, tt), _bf16)


def _projections(h2, seq_len, tables, g_kv_in, w_kva, g_kv_a, g_pre, w_b_in, g_q_a, w_qb, w_k, w_v, *, tt,
                 feature_major):
    n, d = h2.shape
    kv_lora = g_kv_a.shape[0]
    q_lora = g_q_a.shape[0]
    gate_w = w_b_in.shape[1] - q_lora
    assert n % tt == 0
    nt = max(seq_len // tt, 1)
    tok = lambda w: pl.BlockSpec((tt, w), lambda i: (i, 0))
    tab = pl.BlockSpec((tt, LANES), lambda i: (i % nt, 0))
    fm = lambda w: pl.BlockSpec((1, w, tt), lambda i: (i // nt, 0, i % nt))
    half = QK_ROPE // 2
    in_specs = [tok(d), tab, tab, tab]
    if feature_major:
        tab_t = pl.BlockSpec((half, tt), lambda i: (0, i % nt))
        in_specs += [tab, tab_t, tab_t]
    in_specs += [_const_spec((1, d)), _const_spec(w_kva.shape), _const_spec((1, kv_lora)),
                 _const_spec((1, d)), _const_spec(w_b_in.shape), _const_spec((1, q_lora)), _const_spec(w_qb.shape)]
    args = [h2, *tables, g_kv_in[None], w_kva, g_kv_a[None], g_pre[None], w_b_in, g_q_a[None], w_qb]
    out_shape = [jax.ShapeDtypeStruct((n, kv_lora), _f32), jax.ShapeDtypeStruct((n, QK_ROPE), _f32),
                 jax.ShapeDtypeStruct((n, gate_w), _bf16)]
    out_specs = [tok(kv_lora), tok(QK_ROPE), tok(gate_w)]
    if feature_major:
        nb = n // seq_len
        in_specs += [_const_spec(w_k.shape), _const_spec(w_v.shape)]
        args += [w_k, w_v]
        out_shape += [jax.ShapeDtypeStruct((nb, N_HEADS * HEAD_PAD, seq_len), _bf16),
                      jax.ShapeDtypeStruct((n, N_HEADS * HEAD_PAD), _bf16),
                      jax.ShapeDtypeStruct((nb, N_HEADS * V_HEAD, seq_len), _bf16)]
        out_specs += [fm(N_HEADS * HEAD_PAD), tok(N_HEADS * HEAD_PAD), fm(N_HEADS * V_HEAD)]
    else:
        out_shape.append(jax.ShapeDtypeStruct((n, N_HEADS * HEAD_PAD), _bf16))
        out_specs.append(tok(N_HEADS * HEAD_PAD))
    return pl.pallas_call(
        functools.partial(_proj_kernel, kv_lora=kv_lora, q_lora=q_lora, feature_major=feature_major),
        grid=(n // tt,), in_specs=in_specs, out_specs=tuple(out_specs), out_shape=tuple(out_shape),
        compiler_params=pltpu.CompilerParams(dimension_semantics=("arbitrary",), vmem_limit_bytes=VMEM_LIMIT),
        name="projections",
    )(*args)


ATTN_STEPS = 4
ONES_ROWS = 16


def _attn_prompt_kernel(q_ref, k_ref, v_ref, o_ref, s_a, s_b, acc0, acc1, m0, m1, mtab, ones_ref, *, tq):
    t = q_ref.shape[2]
    nq = t // tq
    total = nq * (nq + 1) // 2
    assert total % (2 * ATTN_STEPS) == 0
    n_mask = tq // CHUNK
    pad_rows = HEAD_PAD - MASK_AT
    r = lax.broadcasted_iota(jnp.int32, (pad_rows, tq), 0)
    qc = _chunk_of(lax.broadcasted_iota(jnp.int32, (pad_rows, tq), 1))
    mtab[0] = jnp.zeros((pad_rows, tq), _bf16)
    mtab[1] = jnp.where((r < n_mask) & (r > qc), MASK_NEG, 0.0).astype(_bf16)
    ones_ref[...] = jnp.ones((ONES_ROWS, tq), _bf16)
    accs, ms = (acc0, acc1), (m0, m1)

    for hh in range(2):
        ms[hh][...] = jnp.full(ms[hh].shape, NEG_INF, _f32)
        accs[hh][...] = jnp.zeros(accs[hh].shape, _f32)

    def nxt(pair):
        i, j = pair
        wrap = j + 1 > i
        return jnp.where(wrap, i + 1, i), jnp.where(wrap, 0, j + 1)

    def prefetch(pair, bank, slot):
        i, j = pair
        i = jnp.minimum(i, nq - 1)
        q0 = pl.multiple_of(i * tq, tq)
        k0 = pl.multiple_of(j * tq, tq)
        msel = mtab[jnp.where(i == j, 1, 0)]
        for hh in range(2):
            q_t = jnp.concatenate([q_ref[0, hh * HEAD_PAD:hh * HEAD_PAD + MASK_AT, pl.ds(q0, tq)], msel], axis=0)
            k_t = k_ref[0, pl.ds(k0, tq), hh * HEAD_PAD:(hh + 1) * HEAD_PAD]
            bank[slot, hh] = jnp.dot(k_t, q_t, preferred_element_type=_f32)

    def consume(pair, bank, slot):
        i, j = pair
        k0 = pl.multiple_of(j * tq, tq)
        for hh in range(2):
            m_old = ms[hh][i]
            m_new = jnp.maximum(m_old, jnp.max(bank[slot, hh], axis=0, keepdims=True))
            alpha = jnp.exp2(m_old - m_new)
            p = jnp.exp2(bank[slot, hh] - m_new).astype(_bf16)
            v_aug = jnp.concatenate([v_ref[0, hh * V_HEAD:(hh + 1) * V_HEAD, pl.ds(k0, tq)], ones_ref[...]], axis=0)
            accs[hh][i] = alpha * accs[hh][i] + jnp.dot(v_aug, p, preferred_element_type=_f32)
            ms[hh][i] = m_new

    def stage(first, src, dst):
        pairs = [first]
        for _ in range(2 * ATTN_STEPS - 1):
            pairs.append(nxt(pairs[-1]))
        for s in range(ATTN_STEPS):
            prefetch(pairs[ATTN_STEPS + s], dst, s)
            consume(pairs[s], src, s)
        return pairs[ATTN_STEPS]

    zero = (jnp.int32(0), jnp.int32(0))
    pair = zero
    for s in range(ATTN_STEPS):
        prefetch(pair, s_a, s)
        pair = nxt(pair)

    def body(u, first):
        return stage(stage(first, s_a, s_b), s_b, s_a)

    lax.fori_loop(0, total // (2 * ATTN_STEPS), body, zero)

    def finish(i, carry):
        q0 = pl.multiple_of(i * tq, tq)
        o = [accs[hh][i][:V_HEAD] / accs[hh][i][V_HEAD:V_HEAD + 1] for hh in range(2)]
        o_ref[0, pl.ds(q0, tq), :] = jnp.concatenate(o, axis=0).T.astype(o_ref.dtype)
        return carry

    lax.fori_loop(0, nq, finish, 0)


def _attention_prompt(q_t, k, v_t, *, tq):
    b, _, t = q_t.shape
    assert t % tq == 0 and tq % CHUNK == 0 and tq // CHUNK <= HEAD_PAD - MASK_AT
    n_pairs = N_HEADS // 2
    nq = t // tq
    bank = pltpu.VMEM((ATTN_STEPS, 2, tq, tq), _f32)
    acc = pltpu.VMEM((nq, V_HEAD + ONES_ROWS, tq), _f32)
    mx = pltpu.VMEM((nq, 1, tq), _f32)
    return pl.pallas_call(
        functools.partial(_attn_prompt_kernel, tq=tq),
        grid=(b, n_pairs),
        in_specs=[pl.BlockSpec((1, 2 * HEAD_PAD, t), lambda i, j: (i, j, 0)),
                  pl.BlockSpec((1, t, 2 * HEAD_PAD), lambda i, j: (i, 0, j)),
                  pl.BlockSpec((1, 2 * V_HEAD, t), lambda i, j: (i, j, 0))],
        out_specs=pl.BlockSpec((1, t, 2 * V_HEAD), lambda i, j: (i, 0, j)),
        out_shape=jax.ShapeDtypeStruct((b, t, N_HEADS * V_HEAD), _bf16),
        scratch_shapes=[bank, bank, acc, acc, mx, mx, pltpu.VMEM((2, HEAD_PAD - MASK_AT, tq), _bf16),
                        pltpu.VMEM((ONES_ROWS, tq), _bf16)],
        compiler_params=pltpu.CompilerParams(dimension_semantics=("arbitrary", "arbitrary"),
                                             vmem_limit_bytes=VMEM_LIMIT),
        name="attention_prompt",
    )(q_t, k, v_t)


def _attn_sample_kernel(q_ref, cc_ref, ckr_ref, cn_ref, krn_ref, wuk_ref, wuv_ref, o_ref, *, past_len, mask_new):
    ts = q_ref.shape[1]
    q = q_ref[0]
    q_lat, q_rope = [], []
    for hh in range(N_HEADS):
        qh = q[:, hh * HEAD_PAD:(hh + 1) * HEAD_PAD]
        q_lat.append(jnp.dot(qh[:, :QK_NOPE], wuk_ref[hh], preferred_element_type=_f32).astype(_bf16))
        q_rope.append(qh[:, QK_NOPE:QK_NOPE + QK_ROPE])
    q_lat = jnp.concatenate(q_lat, axis=0)
    q_rope = jnp.concatenate(q_rope, axis=0)
    cc = cc_ref[0].astype(_bf16)
    cn = cn_ref[0].astype(_bf16)
    s_c = (lax.dot_general(q_lat, cc, _NT, preferred_element_type=_f32)
           + lax.dot_general(q_rope, ckr_ref[0].astype(_bf16), _NT, preferred_element_type=_f32))
    s_n = (lax.dot_general(q_lat, cn, _NT, preferred_element_type=_f32)
           + lax.dot_general(q_rope, krn_ref[0].astype(_bf16), _NT, preferred_element_type=_f32))
    if mask_new:
        assert ts & (ts - 1) == 0
        rows = lax.broadcasted_iota(jnp.int32, s_n.shape, 0) & (ts - 1)
        cols = lax.broadcasted_iota(jnp.int32, s_n.shape, 1)
        s_n = jnp.where(_chunk_of(past_len + cols) <= _chunk_of(past_len + rows), s_n, NEG_INF)
    m = jnp.maximum(jnp.max(s_c, axis=1, keepdims=True), jnp.max(s_n, axis=1, keepdims=True))
    p_c = jnp.exp2(s_c - m)
    p_n = jnp.exp2(s_n - m)
    l = jnp.sum(p_c, axis=1, keepdims=True) + jnp.sum(p_n, axis=1, keepdims=True)
    o_lat = (jnp.dot(p_c.astype(_bf16), cc, preferred_element_type=_f32)
             + jnp.dot(p_n.astype(_bf16), cn, preferred_element_type=_f32)) / l
    o_lat = o_lat.astype(_bf16)
    outs = [jnp.dot(o_lat[hh * ts:(hh + 1) * ts], wuv_ref[hh], preferred_element_type=_f32)
            for hh in range(N_HEADS)]
    o_ref[0] = jnp.concatenate(outs, axis=1).astype(o_ref.dtype)


def _attention_sample(q, cache_c, cache_kr, c_new, kr_new, w_uk_t, w_uv):
    b, ts, _ = q.shape
    past = cache_c.shape[1]
    kv_lora = cache_c.shape[2]
    mask_new = (past // CHUNK) != ((past + ts - 1) // CHUNK)
    blk = lambda s, w: pl.BlockSpec((1, s, w), lambda i: (i, 0, 0))
    return pl.pallas_call(
        functools.partial(_attn_sample_kernel, past_len=past, mask_new=mask_new),
        grid=(b,),
        in_specs=[blk(ts, N_HEADS * HEAD_PAD), blk(past, kv_lora), blk(past, QK_ROPE), blk(ts, kv_lora),
                  blk(ts, QK_ROPE), _const_spec(w_uk_t.shape), _const_spec(w_uv.shape)],
        out_specs=blk(ts, N_HEADS * V_HEAD),
        out_shape=jax.ShapeDtypeStruct((b, ts, N_HEADS * V_HEAD), _bf16),
        compiler_params=pltpu.CompilerParams(dimension_semantics=("arbitrary",), vmem_limit_bytes=VMEM_LIMIT),
        name="attention_sample",
    )(q, cache_c, cache_kr, c_new, kr_new, w_uk_t, w_uv)


def _out_kernel(o_ref, gate_ref, h_ref, w_ref, g_ref, y_ref):
    m = (o_ref[...].astype(_f32) * gate_ref[...].astype(_f32)).astype(_bf16)
    out = jnp.dot(m, w_ref[...], preferred_element_type=_f32)
    y_ref[...] = h_ref[...] + _rms(out, g_ref[...])


def _out_proj(o2, gate2, h2, w_out, g_post, *, tt):
    n, d = h2.shape
    w = gate2.shape[1]
    assert n % tt == 0
    return pl.pallas_call(
        _out_kernel,
        grid=(n // tt,),
        in_specs=[pl.BlockSpec((tt, w), lambda i: (i, 0)), pl.BlockSpec((tt, w), lambda i: (i, 0)),
                  pl.BlockSpec((tt, d), lambda i: (i, 0)), _const_spec(w_out.shape), _const_spec((1, d))],
        out_specs=pl.BlockSpec((tt, d), lambda i: (i, 0)),
        out_shape=jax.ShapeDtypeStruct((n, d), _f32),
        compiler_params=pltpu.CompilerParams(dimension_semantics=("arbitrary",), vmem_limit_bytes=VMEM_LIMIT),
        name="out_proj",
    )(o2, gate2, h2, w_out, g_post[None])


def _prep_weights(w_a_in, w_a_out, w_kv_a, w_kv_b, w_b_in, w_q_b, w_b_out):
    kv_lora = w_kv_b.shape[0]
    pad_tail = HEAD_PAD - MASK_AT
    wkva = jnp.concatenate([w_kv_a[:, :kv_lora], jnp.zeros((w_kv_a.shape[0], QK_NOPE), _f32),
                            w_kv_a[:, kv_lora:], jnp.zeros((w_kv_a.shape[0], pad_tail), _f32)], axis=1)
    q_lora = w_q_b.shape[0]
    wqb = w_q_b.reshape(q_lora, N_HEADS, MASK_AT)
    wqb = jnp.pad(wqb, ((0, 0), (0, 0), (0, pad_tail))).reshape(q_lora, N_HEADS * HEAD_PAD)
    wkvb = w_kv_b.reshape(kv_lora, N_HEADS, QK_NOPE + V_HEAD)
    w_uk, w_uv = wkvb[..., :QK_NOPE], wkvb[..., QK_NOPE:]
    w_k = jnp.pad(w_uk, ((0, 0), (0, 0), (0, HEAD_PAD - QK_NOPE))).reshape(kv_lora, N_HEADS * HEAD_PAD)
    w_v = w_uv.reshape(kv_lora, N_HEADS * V_HEAD)
    bf = lambda a: a.astype(_bf16)
    return dict(w_in=bf(w_a_in), w_out_a=bf(w_a_out), wkva=bf(wkva), w_b_in=bf(w_b_in), wqb=bf(wqb),
                wqb_t=bf(wqb.T), w_k=bf(w_k), w_v_t=bf(w_v.T), w_uk_t=bf(jnp.transpose(w_uk, (1, 2, 0))),
                w_uv_h=bf(jnp.transpose(w_uv, (1, 0, 2))), w_out_b=bf(w_b_out))


def _trunk(x, state, past, pw, gains, *, bb_n, tt, tok_tile, tq):
    (g_pre, g_post, w_dw, b_dw, g_ln, b_ln, g_kv_in, g_kv_a, g_q_a) = gains
    b, t, d = x.shape
    prompt = past is None
    offset = 0 if prompt else past[0].shape[1]
    h, conv_state = _conv_layer(x, state, g_pre[0], pw["w_in"], w_dw, b_dw, g_ln, b_ln, pw["w_out_a"], g_post[0],
                                bb_n=bb_n, tt=tt)
    h2 = h.reshape(b * t, d)
    if t >= tok_tile:
        assert t % tok_tile == 0
        tables = _rope_tables(t, t, offset, tq // CHUNK if prompt else 0)
    else:
        assert tok_tile % t == 0 and not prompt
        tables = _rope_tables(tok_tile, t, offset, 0)
    outs = _projections(h2, t, tables, g_kv_in, pw["wkva"], g_kv_a, g_pre[1], pw["w_b_in"], g_q_a,
                        pw["wqb_t"] if prompt else pw["wqb"], pw["w_k"], pw["w_v_t"], tt=tok_tile,
                        feature_major=prompt)
    c3 = outs[0].reshape(b, t, -1)
    kr3 = outs[1].reshape(b, t, -1)
    gate = outs[2]
    if prompt:
        q_t, k, v_t = outs[3:]
        o = _attention_prompt(q_t, k.reshape(b, t, -1), v_t, tq=tq)
    else:
        o = _attention_sample(outs[3].reshape(b, t, -1), past[0], past[1], c3, kr3, pw["w_uk_t"], pw["w_uv_h"])
    y = _out_proj(o.reshape(b * t, -1), gate, h2, pw["w_out_b"], g_post[1], tt=tok_tile)
    return y.reshape(b, t, d), conv_state, c3, kr3


def kernel(x_prompt, x_sample, state_conv, cache_kv_latent, cache_k_rope, g_pre, g_post, w_a_in, w_a_dw, b_a_dw, g_a_ln, b_a_ln, w_a_out, g_kv_in, w_kv_a, g_kv_a, w_kv_b, w_b_in, g_q_a, w_q_b, w_b_out):
    assert w_a_in.shape[0] == 1 and w_b_in.shape[0] == 1, "one conv layer followed by one attention layer"
    pw = _prep_weights(w_a_in[0], w_a_out[0], w_kv_a, w_kv_b, w_b_in[0], w_q_b[0], w_b_out[0])
    gains = (g_pre, g_post, w_a_dw[0], b_a_dw[0], g_a_ln[0], b_a_ln[0], g_kv_in, g_kv_a, g_q_a[0])
    y_p, cs_p, c_p, kr_p = _trunk(x_prompt, None, None, pw, gains, bb_n=1, tt=256, tok_tile=256, tq=256)
    y_s, cs_s, c_s, kr_s = _trunk(x_sample, state_conv, (cache_kv_latent, cache_k_rope), pw, gains,
                                  bb_n=8, tt=x_sample.shape[1], tok_tile=256, tq=256)
    return (y_p, y_s, cs_p, c_p, kr_p, cs_s, c_s, kr_s)
```

```python
import functools
import math

import jax
import jax.numpy as jnp
from jax import lax
from jax.experimental import pallas as pl
from jax.experimental.pallas import tpu as pltpu

CHUNK = 64
N_HEADS = 16
QK_NOPE = 64
QK_ROPE = 32
V_HEAD = 64
ROPE_THETA = 10000.0
NORM_EPS = 1e-6
ATTN_SCALE = 1.0 / math.sqrt(QK_NOPE + QK_ROPE)
LOG2E = 1.4426950408889634
Q_SCALE = ATTN_SCALE * LOG2E

LANES = 128
SUBLANES = 8
HEAD_PAD = 128
MASK_AT = QK_NOPE + QK_ROPE
HIST = 32
MIN_CHAIN_ROWS = 128
VMEM_LIMIT = 56 * 1024 * 1024
NEG_INF = float("-inf")
MASK_NEG = -1e30

_f32 = jnp.float32
_bf16 = jnp.bfloat16
_NT = (((1,), (1,)), ((), ()))


def _chunk_of(pos):
    assert CHUNK & (CHUNK - 1) == 0
    return lax.shift_right_logical(pos, int(math.log2(CHUNK)))


def _rms(x, g):
    ms = jnp.mean(x * x, axis=-1, keepdims=True)
    return x * lax.rsqrt(ms + NORM_EPS) * g


def _const_spec(shape):
    nd = len(shape)
    return pl.BlockSpec(shape, lambda *_: (0,) * nd, pipeline_mode=pl.Buffered(1))


def _rope_table_kernel(*refs, period, offset, n_mask):
    if n_mask:
        inv_ref, invc_ref, c_ref, sa_ref, sb_ref, e_ref, cost_ref, sint_ref = refs
    else:
        inv_ref, c_ref, sa_ref, sb_ref = refs
    rows = c_ref.shape[0]
    row = lax.broadcasted_iota(jnp.int32, (rows, LANES), 0)
    lane = lax.broadcasted_iota(jnp.int32, (rows, LANES), 1)
    if period < rows:
        assert period & (period - 1) == 0
        row = row & (period - 1)
    ang = (offset + row).astype(_f32) * inv_ref[...]
    cos = jnp.cos(ang)
    sin = jnp.sin(ang)
    half = QK_ROPE // 2
    in_x1 = (lane >= QK_NOPE) & (lane < QK_NOPE + half)
    in_x2 = (lane >= QK_NOPE + half) & (lane < QK_NOPE + QK_ROPE)
    c_ref[...] = jnp.where(lane < QK_NOPE, 1.0, jnp.where(in_x1 | in_x2, cos, 0.0))
    sa_ref[...] = jnp.where(in_x1, -sin, 0.0)
    sb_ref[...] = jnp.where(in_x2, sin, 0.0)
    if n_mask:
        assert period >= rows and n_mask & (n_mask - 1) == 0
        e_ref[...] = jnp.where(lane - MASK_AT == (_chunk_of(offset + row) & (n_mask - 1)), 1.0, 0.0)
        pos_l = (offset + lax.broadcasted_iota(jnp.int32, (half, rows), 1)).astype(_f32)
        ang_t = invc_ref[...] * pos_l
        cost_ref[...] = jnp.cos(ang_t) * Q_SCALE
        sint_ref[...] = jnp.sin(ang_t) * Q_SCALE


def _rope_tables(rows, period, offset, n_mask):
    half = QK_ROPE // 2
    inv = ROPE_THETA ** (-jnp.arange(half, dtype=_f32) / half)
    inv_l = jnp.tile(inv, LANES // half)[None, :]
    out = jax.ShapeDtypeStruct((rows, LANES), _f32)
    if n_mask:
        out_t = jax.ShapeDtypeStruct((half, rows), _f32)
        out_shape, args = (out, out, out, out, out_t, out_t), (inv_l, inv[:, None])
    else:
        out_shape, args = (out, out, out), (inv_l,)
    return pl.pallas_call(
        functools.partial(_rope_table_kernel, period=period, offset=offset, n_mask=n_mask),
        out_shape=out_shape,
        name="rope_tables",
    )(*args)


def _conv_layer_kernel(*refs, bb_n, tt, width, has_state, n_chain):
    if has_state:
        (x_ref, st_ref, gpre_ref, win_ref, wdw_ref, bdw_ref, gln_ref, bln_ref, wout_ref, gpost_ref,
         h_ref, cst_ref, vbuf, ybuf) = refs
    else:
        (x_ref, gpre_ref, win_ref, wdw_ref, bdw_ref, gln_ref, bln_ref, wout_ref, gpost_ref,
         h_ref, cst_ref, vbuf, ybuf) = refs
        st_ref = None
    d = x_ref.shape[-1]
    c = wdw_ref.shape[-1]
    n_cb = c // LANES
    j = pl.program_id(1)
    pad = HIST - (width - 1)
    per = bb_n // n_chain
    rows = per * tt

    @pl.when(j == 0)
    def _():
        for bb in range(bb_n):
            for cb in range(n_cb):
                if has_state:
                    vbuf[bb, cb, 0:SUBLANES, :] = jnp.zeros((SUBLANES, LANES), _f32)
                    vbuf[bb, cb, pad:HIST, :] = st_ref[0, bb, :, cb * LANES:(cb + 1) * LANES]
                else:
                    vbuf[bb, cb, 0:HIST, :] = jnp.zeros((HIST, LANES), _f32)

    xs, zs = [], []
    for ch in range(n_chain):
        x = x_ref[ch * per:(ch + 1) * per].reshape(rows, d)
        u = _rms(x, gpre_ref[...]).astype(_bf16)
        p = jnp.dot(u, win_ref[...], preferred_element_type=_f32)
        v = p[:, :c] * jax.nn.sigmoid(p[:, c:2 * c])
        xs.append(x)
        zs.append(p[:, 2 * c:])
        for b1 in range(per):
            for cb in range(n_cb):
                vbuf[ch * per + b1, cb, HIST:HIST + tt, :] = v[b1 * tt:(b1 + 1) * tt, cb * LANES:(cb + 1) * LANES]

    rc = min(tt, 128)
    nr = rc // SUBLANES
    for cb in range(n_cb):
        lanes = pl.ds(cb * LANES, LANES)
        wb = [jnp.broadcast_to(wdw_ref[k:k + 1, lanes], (SUBLANES, LANES)) for k in range(width)]
        bias = jnp.broadcast_to(bdw_ref[0:1, lanes], (SUBLANES, LANES))
        for bb in range(bb_n):
            def chunk(i, carry, bb=bb, cb=cb, lanes=lanes, wb=wb, bias=bias):
                r0 = pl.multiple_of(i * rc, rc)
                for r in range(nr):
                    acc = bias
                    for k in range(width):
                        acc = acc + wb[k] * vbuf[bb, cb, pl.ds(r0 + SUBLANES * r + pad + k, SUBLANES, stride=1), :]
                    ybuf[pl.ds(bb * tt + r0 + SUBLANES * r, SUBLANES), lanes] = acc
                return carry
            lax.fori_loop(0, tt // rc, chunk, 0)

    for bb in range(bb_n):
        for cb in range(n_cb):
            cst_ref[0, bb, :, cb * LANES:(cb + 1) * LANES] = vbuf[bb, cb, tt + pad:tt + HIST, :]
            vbuf[bb, cb, 0:HIST, :] = vbuf[bb, cb, tt:tt + HIST, :]

    for ch in range(n_chain):
        y = ybuf[ch * rows:(ch + 1) * rows, :]
        mu = jnp.mean(y, axis=-1, keepdims=True)
        yc = y - mu
        var = jnp.mean(yc * yc, axis=-1, keepdims=True)
        yn = yc * lax.rsqrt(var + NORM_EPS) * gln_ref[...] + bln_ref[...]
        m = (jax.nn.silu(yn) * jax.nn.silu(zs[ch])).astype(_bf16)
        out = jnp.dot(m, wout_ref[...], preferred_element_type=_f32)
        h = xs[ch] + _rms(out, gpost_ref[...])
        h_ref[ch * per:(ch + 1) * per] = h.reshape(per, tt, d)


def _conv_layer(x, state, g_pre, w_in, w_dw, b_dw, g_ln, b_ln, w_out, g_post, *, bb_n, tt):
    b, t, d = x.shape
    width, c = w_dw.shape
    has_state = state is not None
    assert t % tt == 0 and b % bb_n == 0 and tt % SUBLANES == 0 and tt >= HIST and c % LANES == 0
    assert width - 1 <= HIST
    grid = (b // bb_n, t // tt)
    n_chain = bb_n if tt >= MIN_CHAIN_ROWS else 1
    in_specs = [pl.BlockSpec((bb_n, tt, d), lambda i, j: (i, j, 0))]
    args = [x]
    if has_state:
        in_specs.append(pl.BlockSpec((1, bb_n, width - 1, c), lambda i, j: (0, i, 0, 0)))
        args.append(state)
    in_specs += [_const_spec((1, d)), _const_spec(w_in.shape), _const_spec(w_dw.shape), _const_spec((1, c)),
                 _const_spec((1, c)), _const_spec((1, c)), _const_spec(w_out.shape), _const_spec((1, d))]
    args += [g_pre[None], w_in, w_dw, b_dw[None], g_ln[None], b_ln[None], w_out, g_post[None]]
    out_shape = (jax.ShapeDtypeStruct((b, t, d), _f32),
                 jax.ShapeDtypeStruct((1, b, width - 1, c), _f32))
    out_specs = (pl.BlockSpec((bb_n, tt, d), lambda i, j: (i, j, 0)),
                 pl.BlockSpec((1, bb_n, width - 1, c), lambda i, j: (0, i, 0, 0)))
    return pl.pallas_call(
        functools.partial(_conv_layer_kernel, bb_n=bb_n, tt=tt, width=width, has_state=has_state,
                          n_chain=n_chain),
        grid=grid, in_specs=in_specs, out_specs=out_specs, out_shape=out_shape,
        scratch_shapes=[pltpu.VMEM((bb_n, c // LANES, HIST + tt, LANES), _f32),
                        pltpu.VMEM((bb_n * tt, c), _f32)],
        compiler_params=pltpu.CompilerParams(dimension_semantics=("arbitrary", "arbitrary"),
                                             vmem_limit_bytes=VMEM_LIMIT),
        name="conv_layer",
    )(*args)


def _rope_block(x, c_t, sa_t, sb_t):
    half = QK_ROPE // 2
    return x * c_t + pltpu.roll(x, LANES - half, 1) * sa_t + pltpu.roll(x, half, 1) * sb_t


def _proj_kernel(*refs, kv_lora, feature_major):
    if feature_major:
        (h_ref, ct_ref, sat_ref, sbt_ref, et_ref, cost_ref, sint_ref, gkv_ref, wkva_ref, gkva_ref, gpre_ref,
         wqa_ref, gqa_ref, wqb_ref, wk_ref, wv_ref, c_ref, kr_ref, q_ref, k_ref, v_ref) = refs
    else:
        (h_ref, ct_ref, sat_ref, sbt_ref, gkv_ref, wkva_ref, gkva_ref, gpre_ref,
         wqa_ref, gqa_ref, wqb_ref, c_ref, kr_ref, q_ref) = refs
    h = h_ref[...]
    hn = h * lax.rsqrt(jnp.mean(h * h, axis=-1, keepdims=True) + NORM_EPS)
    c_t, sa_t, sb_t = ct_ref[...], sat_ref[...], sbt_ref[...]

    lat = jnp.dot((hn * gkv_ref[...]).astype(_bf16), wkva_ref[...], preferred_element_type=_f32)
    cl = _rms(lat[:, :kv_lora], gkva_ref[...])
    c_ref[...] = cl
    kr_blk = _rope_block(lat[:, kv_lora:kv_lora + HEAD_PAD], c_t, sa_t, sb_t)
    kr_ref[...] = kr_blk[:, QK_NOPE:QK_NOPE + QK_ROPE]

    qa = jnp.dot((hn * gpre_ref[...]).astype(_bf16), wqa_ref[...], preferred_element_type=_f32)
    qn = _rms(qa, gqa_ref[...]).astype(_bf16)

    if not feature_major:
        q = jnp.dot(qn, wqb_ref[...], preferred_element_type=_f32) * Q_SCALE
        for hh in range(N_HEADS):
            sl = slice(hh * HEAD_PAD, (hh + 1) * HEAD_PAD)
            q_ref[:, sl] = _rope_block(q[:, sl], c_t, sa_t, sb_t).astype(_bf16)
        return

    cb = cl.astype(_bf16)
    k = jnp.dot(cb, wk_ref[...], preferred_element_type=_f32)
    kr_e = kr_blk + et_ref[...]
    for hh in range(N_HEADS):
        sl = slice(hh * HEAD_PAD, (hh + 1) * HEAD_PAD)
        k_ref[:, sl] = (k[:, sl] + kr_e).astype(_bf16)
    v_ref[0] = lax.dot_general(wv_ref[...], cb, _NT, preferred_element_type=_f32).astype(_bf16)

    q_t = lax.dot_general(wqb_ref[...], qn, _NT, preferred_element_type=_f32)
    cos_t, sin_t = cost_ref[...], sint_ref[...]
    half = QK_ROPE // 2
    tt = q_t.shape[1]
    for hh in range(N_HEADS):
        b0 = hh * HEAD_PAD
        x1 = q_t[b0 + QK_NOPE:b0 + QK_NOPE + half]
        x2 = q_t[b0 + QK_NOPE + half:b0 + MASK_AT]
        q_ref[0, b0:b0 + QK_NOPE, :] = (q_t[b0:b0 + QK_NOPE] * Q_SCALE).astype(_bf16)
        q_ref[0, b0 + QK_NOPE:b0 + QK_NOPE + half, :] = (x1 * cos_t - x2 * sin_t).astype(_bf16)
        q_ref[0, b0 + QK_NOPE + half:b0 + MASK_AT, :] = (x2 * cos_t + x1 * sin_t).astype(_bf16)
        q_ref[0, b0 + MASK_AT:b0 + HEAD_PAD, :] = jnp.zeros((HEAD_PAD - MASK_AT, tt), _bf16)


def _projections(h2, seq_len, tables, g_kv_in, w_kva, g_kv_a, g_pre, w_qa, g_q_a, w_qb, w_k, w_v, *, tt,
                 feature_major):
    n, d = h2.shape
    kv_lora = g_kv_a.shape[0]
    q_lora = g_q_a.shape[0]
    assert n % tt == 0 and w_qa.shape[1] == q_lora
    nt = max(seq_len // tt, 1)
    tok = lambda w: pl.BlockSpec((tt, w), lambda i: (i, 0))
    tab = pl.BlockSpec((tt, LANES), lambda i: (i % nt, 0))
    fm = lambda w: pl.BlockSpec((1, w, tt), lambda i: (i // nt, 0, i % nt))
    half = QK_ROPE // 2
    in_specs = [tok(d), tab, tab, tab]
    if feature_major:
        tab_t = pl.BlockSpec((half, tt), lambda i: (0, i % nt))
        in_specs += [tab, tab_t, tab_t]
    in_specs += [_const_spec((1, d)), _const_spec(w_kva.shape), _const_spec((1, kv_lora)),
                 _const_spec((1, d)), _const_spec(w_qa.shape), _const_spec((1, q_lora)), _const_spec(w_qb.shape)]
    args = [h2, *tables, g_kv_in[None], w_kva, g_kv_a[None], g_pre[None], w_qa, g_q_a[None], w_qb]
    out_shape = [jax.ShapeDtypeStruct((n, kv_lora), _f32), jax.ShapeDtypeStruct((n, QK_ROPE), _f32)]
    out_specs = [tok(kv_lora), tok(QK_ROPE)]
    if feature_major:
        nb = n // seq_len
        in_specs += [_const_spec(w_k.shape), _const_spec(w_v.shape)]
        args += [w_k, w_v]
        out_shape += [jax.ShapeDtypeStruct((nb, N_HEADS * HEAD_PAD, seq_len), _bf16),
                      jax.ShapeDtypeStruct((n, N_HEADS * HEAD_PAD), _bf16),
                      jax.ShapeDtypeStruct((nb, N_HEADS * V_HEAD, seq_len), _bf16)]
        out_specs += [fm(N_HEADS * HEAD_PAD), tok(N_HEADS * HEAD_PAD), fm(N_HEADS * V_HEAD)]
    else:
        out_shape.append(jax.ShapeDtypeStruct((n, N_HEADS * HEAD_PAD), _bf16))
        out_specs.append(tok(N_HEADS * HEAD_PAD))
    return pl.pallas_call(
        functools.partial(_proj_kernel, kv_lora=kv_lora, feature_major=feature_major),
        grid=(n // tt,), in_specs=in_specs, out_specs=tuple(out_specs), out_shape=tuple(out_shape),
        compiler_params=pltpu.CompilerParams(dimension_semantics=("arbitrary",), vmem_limit_bytes=VMEM_LIMIT),
        name="projections",
    )(*args)


ATTN_STEPS = 8
ONES_ROWS = 16


def _attn_prompt_kernel(q_ref, k_ref, v_ref, o_ref, s_a, s_b, acc0, acc1, m0, m1, mtab, ones_ref, *, tq):
    t = q_ref.shape[2]
    nq = t // tq
    total = nq * (nq + 1) // 2
    assert total % ATTN_STEPS == 0
    n_mask = tq // CHUNK
    pad_rows = HEAD_PAD - MASK_AT
    r = lax.broadcasted_iota(jnp.int32, (pad_rows, tq), 0)
    qc = _chunk_of(lax.broadcasted_iota(jnp.int32, (pad_rows, tq), 1))
    mtab[0] = jnp.zeros((pad_rows, tq), _bf16)
    mtab[1] = jnp.where((r < n_mask) & (r > qc), MASK_NEG, 0.0).astype(_bf16)
    ones_ref[...] = jnp.ones((ONES_ROWS, tq), _bf16)
    accs, ms = (acc0, acc1), (m0, m1)

    for hh in range(2):
        ms[hh][...] = jnp.full(ms[hh].shape, NEG_INF, _f32)
        accs[hh][...] = jnp.zeros(accs[hh].shape, _f32)

    def nxt(pair):
        i, j = pair
        wrap = j + 1 > i
        return jnp.where(wrap, i + 1, i), jnp.where(wrap, 0, j + 1)

    def prefetch(pair, bank, slot):
        i, j = pair
        i = jnp.minimum(i, nq - 1)
        q0 = pl.multiple_of(i * tq, tq)
        k0 = pl.multiple_of(j * tq, tq)
        msel = mtab[jnp.where(i == j, 1, 0)]
        for hh in range(2):
            q_t = jnp.concatenate([q_ref[0, hh * HEAD_PAD:hh * HEAD_PAD + MASK_AT, pl.ds(q0, tq)], msel], axis=0)
            k_t = k_ref[0, pl.ds(k0, tq), hh * HEAD_PAD:(hh + 1) * HEAD_PAD]
            bank[slot, hh] = jnp.dot(k_t, q_t, preferred_element_type=_f32)

    def consume(pair, bank, slot):
        i, j = pair
        k0 = pl.multiple_of(j * tq, tq)
        for hh in range(2):
            m_old = ms[hh][i]
            m_new = jnp.maximum(m_old, jnp.max(bank[slot, hh], axis=0, keepdims=True))
            alpha = jnp.exp2(m_old - m_new)
            p = jnp.exp2(bank[slot, hh] - m_new).astype(_bf16)
            v_aug = jnp.concatenate([v_ref[0, hh * V_HEAD:(hh + 1) * V_HEAD, pl.ds(k0, tq)], ones_ref[...]], axis=0)
            accs[hh][i] = alpha * accs[hh][i] + jnp.dot(v_aug, p, preferred_element_type=_f32)
            ms[hh][i] = m_new

    def stage(first, src, dst):
        pairs = [first]
        for _ in range(2 * ATTN_STEPS - 1):
            pairs.append(nxt(pairs[-1]))
        for s in range(ATTN_STEPS):
            prefetch(pairs[ATTN_STEPS + s], dst, s)
            consume(pairs[s], src, s)
        return pairs[ATTN_STEPS]

    zero = (jnp.int32(0), jnp.int32(0))
    pair = zero
    for s in range(ATTN_STEPS):
        prefetch(pair, s_a, s)
        pair = nxt(pair)

    def body(u, first):
        return stage(stage(first, s_a, s_b), s_b, s_a)

    n_stages = total // ATTN_STEPS
    first = lax.fori_loop(0, n_stages // 2, body, zero)
    if n_stages % 2:
        stage(first, s_a, s_b)

    def finish(i, carry):
        q0 = pl.multiple_of(i * tq, tq)
        o = [accs[hh][i][:V_HEAD] / accs[hh][i][V_HEAD:V_HEAD + 1] for hh in range(2)]
        o_ref[0, pl.ds(q0, tq), :] = jnp.concatenate(o, axis=0).T.astype(o_ref.dtype)
        return carry

    lax.fori_loop(0, nq, finish, 0)


def _attention_prompt(q_t, k, v_t, *, tq):
    b, _, t = q_t.shape
    assert t % tq == 0 and tq % CHUNK == 0 and tq // CHUNK <= HEAD_PAD - MASK_AT
    n_pairs = N_HEADS // 2
    nq = t // tq
    bank = pltpu.VMEM((ATTN_STEPS, 2, tq, tq), _f32)
    acc = pltpu.VMEM((nq, V_HEAD + ONES_ROWS, tq), _f32)
    mx = pltpu.VMEM((nq, 1, tq), _f32)
    return pl.pallas_call(
        functools.partial(_attn_prompt_kernel, tq=tq),
        grid=(b, n_pairs),
        in_specs=[pl.BlockSpec((1, 2 * HEAD_PAD, t), lambda i, j: (i, j, 0)),
                  pl.BlockSpec((1, t, 2 * HEAD_PAD), lambda i, j: (i, 0, j)),
                  pl.BlockSpec((1, 2 * V_HEAD, t), lambda i, j: (i, j, 0))],
        out_specs=pl.BlockSpec((1, t, 2 * V_HEAD), lambda i, j: (i, 0, j)),
        out_shape=jax.ShapeDtypeStruct((b, t, N_HEADS * V_HEAD), _bf16),
        scratch_shapes=[bank, bank, acc, acc, mx, mx, pltpu.VMEM((2, HEAD_PAD - MASK_AT, tq), _bf16),
                        pltpu.VMEM((ONES_ROWS, tq), _bf16)],
        compiler_params=pltpu.CompilerParams(dimension_semantics=("arbitrary", "arbitrary"),
                                             vmem_limit_bytes=VMEM_LIMIT),
        name="attention_prompt",
    )(q_t, k, v_t)


def _attn_sample_kernel(q_ref, cc_ref, ckr_ref, cn_ref, krn_ref, wuk_ref, wuv_ref, o_ref, *, past_len, mask_new):
    ts = q_ref.shape[1]
    q = q_ref[0]
    q_lat, q_rope = [], []
    for hh in range(N_HEADS):
        qh = q[:, hh * HEAD_PAD:(hh + 1) * HEAD_PAD]
        q_lat.append(jnp.dot(qh[:, :QK_NOPE], wuk_ref[hh], preferred_element_type=_f32).astype(_bf16))
        q_rope.append(qh[:, QK_NOPE:QK_NOPE + QK_ROPE])
    q_lat = jnp.concatenate(q_lat, axis=0)
    q_rope = jnp.concatenate(q_rope, axis=0)
    cc = cc_ref[0].astype(_bf16)
    cn = cn_ref[0].astype(_bf16)
    s_c = (lax.dot_general(q_lat, cc, _NT, preferred_element_type=_f32)
           + lax.dot_general(q_rope, ckr_ref[0].astype(_bf16), _NT, preferred_element_type=_f32))
    s_n = (lax.dot_general(q_lat, cn, _NT, preferred_element_type=_f32)
           + lax.dot_general(q_rope, krn_ref[0].astype(_bf16), _NT, preferred_element_type=_f32))
    if mask_new:
        assert ts & (ts - 1) == 0
        rows = lax.broadcasted_iota(jnp.int32, s_n.shape, 0) & (ts - 1)
        cols = lax.broadcasted_iota(jnp.int32, s_n.shape, 1)
        s_n = jnp.where(_chunk_of(past_len + cols) <= _chunk_of(past_len + rows), s_n, NEG_INF)
    m = jnp.maximum(jnp.max(s_c, axis=1, keepdims=True), jnp.max(s_n, axis=1, keepdims=True))
    p_c = jnp.exp2(s_c - m)
    p_n = jnp.exp2(s_n - m)
    l = jnp.sum(p_c, axis=1, keepdims=True) + jnp.sum(p_n, axis=1, keepdims=True)
    o_lat = (jnp.dot(p_c.astype(_bf16), cc, preferred_element_type=_f32)
             + jnp.dot(p_n.astype(_bf16), cn, preferred_element_type=_f32)) / l
    o_lat = o_lat.astype(_bf16)
    outs = [jnp.dot(o_lat[hh * ts:(hh + 1) * ts], wuv_ref[hh], preferred_element_type=_f32)
            for hh in range(N_HEADS)]
    o_ref[0] = jnp.concatenate(outs, axis=1).astype(o_ref.dtype)


def _attention_sample(q, cache_c, cache_kr, c_new, kr_new, w_uk_t, w_uv):
    b, ts, _ = q.shape
    past = cache_c.shape[1]
    kv_lora = cache_c.shape[2]
    mask_new = (past // CHUNK) != ((past + ts - 1) // CHUNK)
    blk = lambda s, w: pl.BlockSpec((1, s, w), lambda i: (i, 0, 0))
    return pl.pallas_call(
        functools.partial(_attn_sample_kernel, past_len=past, mask_new=mask_new),
        grid=(b,),
        in_specs=[blk(ts, N_HEADS * HEAD_PAD), blk(past, kv_lora), blk(past, QK_ROPE), blk(ts, kv_lora),
                  blk(ts, QK_ROPE), _const_spec(w_uk_t.shape), _const_spec(w_uv.shape)],
        out_specs=blk(ts, N_HEADS * V_HEAD),
        out_shape=jax.ShapeDtypeStruct((b, ts, N_HEADS * V_HEAD), _bf16),
        compiler_params=pltpu.CompilerParams(dimension_semantics=("arbitrary",), vmem_limit_bytes=VMEM_LIMIT),
        name="attention_sample",
    )(q, cache_c, cache_kr, c_new, kr_new, w_uk_t, w_uv)


def _out_kernel(o_ref, h_ref, gpre_ref, wz_ref, w_ref, g_ref, y_ref):
    h = h_ref[...]
    z = jnp.dot(_rms(h, gpre_ref[...]).astype(_bf16), wz_ref[...], preferred_element_type=_f32)
    m = (o_ref[...].astype(_f32) * jax.nn.silu(z)).astype(_bf16)
    out = jnp.dot(m, w_ref[...], preferred_element_type=_f32)
    y_ref[...] = h + _rms(out, g_ref[...])


def _out_proj(o2, h2, g_pre, w_z, w_out, g_post, *, tt):
    n, d = h2.shape
    w = o2.shape[1]
    assert n % tt == 0 and w_z.shape == (d, w)
    return pl.pallas_call(
        _out_kernel,
        grid=(n // tt,),
        in_specs=[pl.BlockSpec((tt, w), lambda i: (i, 0)), pl.BlockSpec((tt, d), lambda i: (i, 0)),
                  _const_spec((1, d)), _const_spec(w_z.shape), _const_spec(w_out.shape), _const_spec((1, d))],
        out_specs=pl.BlockSpec((tt, d), lambda i: (i, 0)),
        out_shape=jax.ShapeDtypeStruct((n, d), _f32),
        compiler_params=pltpu.CompilerParams(dimension_semantics=("arbitrary",), vmem_limit_bytes=VMEM_LIMIT),
        name="out_proj",
    )(o2, h2, g_pre[None], w_z, w_out, g_post[None])


def _prep_weights(w_a_in, w_a_out, w_kv_a, w_kv_b, w_b_in, w_q_b, w_b_out):
    kv_lora = w_kv_b.shape[0]
    pad_tail = HEAD_PAD - MASK_AT
    wkva = jnp.concatenate([w_kv_a[:, :kv_lora], jnp.zeros((w_kv_a.shape[0], QK_NOPE), _f32),
                            w_kv_a[:, kv_lora:], jnp.zeros((w_kv_a.shape[0], pad_tail), _f32)], axis=1)
    q_lora = w_q_b.shape[0]
    wqb = w_q_b.reshape(q_lora, N_HEADS, MASK_AT)
    wqb = jnp.pad(wqb, ((0, 0), (0, 0), (0, pad_tail))).reshape(q_lora, N_HEADS * HEAD_PAD)
    wkvb = w_kv_b.reshape(kv_lora, N_HEADS, QK_NOPE + V_HEAD)
    w_uk, w_uv = wkvb[..., :QK_NOPE], wkvb[..., QK_NOPE:]
    w_k = jnp.pad(w_uk, ((0, 0), (0, 0), (0, HEAD_PAD - QK_NOPE))).reshape(kv_lora, N_HEADS * HEAD_PAD)
    w_v = w_uv.reshape(kv_lora, N_HEADS * V_HEAD)
    bf = lambda a: a.astype(_bf16)
    return dict(w_in=bf(w_a_in), w_out_a=bf(w_a_out), wkva=bf(wkva), w_qa=bf(w_b_in[:, :q_lora]),
                w_z=bf(w_b_in[:, q_lora:]), wqb=bf(wqb),
                wqb_t=bf(wqb.T), w_k=bf(w_k), w_v_t=bf(w_v.T), w_uk_t=bf(jnp.transpose(w_uk, (1, 2, 0))),
                w_uv_h=bf(jnp.transpose(w_uv, (1, 0, 2))), w_out_b=bf(w_b_out))


def _trunk(x, state, past, pw, gains, *, bb_n, tt, tok_tile, tq):
    (g_pre, g_post, w_dw, b_dw, g_ln, b_ln, g_kv_in, g_kv_a, g_q_a) = gains
    b, t, d = x.shape
    prompt = past is None
    offset = 0 if prompt else past[0].shape[1]
    h, conv_state = _conv_layer(x, state, g_pre[0], pw["w_in"], w_dw, b_dw, g_ln, b_ln, pw["w_out_a"], g_post[0],
                                bb_n=bb_n, tt=tt)
    h2 = h.reshape(b * t, d)
    if t >= tok_tile:
        assert t % tok_tile == 0
        tables = _rope_tables(t, t, offset, tq // CHUNK if prompt else 0)
    else:
        assert tok_tile % t == 0 and not prompt
        tables = _rope_tables(tok_tile, t, offset, 0)
    outs = _projections(h2, t, tables, g_kv_in, pw["wkva"], g_kv_a, g_pre[1], pw["w_qa"], g_q_a,
                        pw["wqb_t"] if prompt else pw["wqb"], pw["w_k"], pw["w_v_t"], tt=tok_tile,
                        feature_major=prompt)
    c3 = outs[0].reshape(b, t, -1)
    kr3 = outs[1].reshape(b, t, -1)
    if prompt:
        q_t, k, v_t = outs[2:]
        o = _attention_prompt(q_t, k.reshape(b, t, -1), v_t, tq=tq)
    else:
        o = _attention_sample(outs[2].reshape(b, t, -1), past[0], past[1], c3, kr3, pw["w_uk_t"], pw["w_uv_h"])
    y = _out_proj(o.reshape(b * t, -1), h2, g_pre[1], pw["w_z"], pw["w_out_b"], g_post[1], tt=tok_tile)
    return y.reshape(b, t, d), conv_state, c3, kr3


def kernel(x_prompt, x_sample, state_conv, cache_kv_latent, cache_k_rope, g_pre, g_post, w_a_in, w_a_dw, b_a_dw, g_a_ln, b_a_ln, w_a_out, g_kv_in, w_kv_a, g_kv_a, w_kv_b, w_b_in, g_q_a, w_q_b, w_b_out):
    assert w_a_in.shape[0] == 1 and w_b_in.shape[0] == 1, "one conv layer followed by one attention layer"
    pw = _prep_weights(w_a_in[0], w_a_out[0], w_kv_a, w_kv_b, w_b_in[0], w_q_b[0], w_b_out[0])
    gains = (g_pre, g_post, w_a_dw[0], b_a_dw[0], g_a_ln[0], b_a_ln[0], g_kv_in, g_kv_a, g_q_a[0])
    y_p, cs_p, c_p, kr_p = _trunk(x_prompt, None, None, pw, gains, bb_n=2, tt=256, tok_tile=256, tq=256)
    y_s, cs_s, c_s, kr_s = _trunk(x_sample, state_conv, (cache_kv_latent, cache_k_rope), pw, gains,
                                  bb_n=8, tt=x_sample.shape[1], tok_tile=256, tq=256)
    return (y_p, y_s, cs_p, c_p, kr_p, cs_s, c_s, kr_s)
```

```python
import functools
import math

import jax
import jax.numpy as jnp
from jax import lax
from jax.experimental import pallas as pl
from jax.experimental.pallas import tpu as pltpu

CHUNK = 64
N_HEADS = 16
QK_NOPE = 64
QK_ROPE = 32
V_HEAD = 64
ROPE_THETA = 10000.0
NORM_EPS = 1e-6
ATTN_SCALE = 1.0 / math.sqrt(QK_NOPE + QK_ROPE)
LOG2E = 1.4426950408889634
Q_SCALE = ATTN_SCALE * LOG2E

LANES = 128
SUBLANES = 8
HEAD_PAD = 128
MASK_AT = QK_NOPE + QK_ROPE
HIST = 32
MIN_CHAIN_ROWS = 128
CHAIN_ROWS = 256
OUT_TILE = 512
VMEM_LIMIT = 56 * 1024 * 1024
NEG_INF = float("-inf")
MASK_NEG = -1e30

_f32 = jnp.float32
_bf16 = jnp.bfloat16
_NT = (((1,), (1,)), ((), ()))


def _chunk_of(pos):
    assert CHUNK & (CHUNK - 1) == 0
    return lax.shift_right_logical(pos, int(math.log2(CHUNK)))


def _rms(x, g):
    ms = jnp.mean(x * x, axis=-1, keepdims=True)
    return x * lax.rsqrt(ms + NORM_EPS) * g


def _const_spec(shape):
    nd = len(shape)
    return pl.BlockSpec(shape, lambda *_: (0,) * nd, pipeline_mode=pl.Buffered(1))


def _rope_table_kernel(*refs, period, offset, n_mask):
    if n_mask:
        inv_ref, invc_ref, c_ref, sa_ref, sb_ref, e_ref, cost_ref, sint_ref = refs
    else:
        inv_ref, c_ref, sa_ref, sb_ref = refs
    rows = c_ref.shape[0]
    row = lax.broadcasted_iota(jnp.int32, (rows, LANES), 0)
    lane = lax.broadcasted_iota(jnp.int32, (rows, LANES), 1)
    if period < rows:
        assert period & (period - 1) == 0
        row = row & (period - 1)
    ang = (offset + row).astype(_f32) * inv_ref[...]
    cos = jnp.cos(ang)
    sin = jnp.sin(ang)
    half = QK_ROPE // 2
    in_x1 = (lane >= QK_NOPE) & (lane < QK_NOPE + half)
    in_x2 = (lane >= QK_NOPE + half) & (lane < QK_NOPE + QK_ROPE)
    c_ref[...] = jnp.where(lane < QK_NOPE, 1.0, jnp.where(in_x1 | in_x2, cos, 0.0))
    sa_ref[...] = jnp.where(in_x1, -sin, 0.0)
    sb_ref[...] = jnp.where(in_x2, sin, 0.0)
    if n_mask:
        assert period >= rows and n_mask & (n_mask - 1) == 0
        e_ref[...] = jnp.where(lane - MASK_AT == (_chunk_of(offset + row) & (n_mask - 1)), 1.0, 0.0)
        pos_l = (offset + lax.broadcasted_iota(jnp.int32, (half, rows), 1)).astype(_f32)
        ang_t = invc_ref[...] * pos_l
        cost_ref[...] = jnp.cos(ang_t) * Q_SCALE
        sint_ref[...] = jnp.sin(ang_t) * Q_SCALE


def _rope_tables(rows, period, offset, n_mask):
    half = QK_ROPE // 2
    inv = ROPE_THETA ** (-jnp.arange(half, dtype=_f32) / half)
    inv_l = jnp.tile(inv, LANES // half)[None, :]
    out = jax.ShapeDtypeStruct((rows, LANES), _f32)
    if n_mask:
        out_t = jax.ShapeDtypeStruct((half, rows), _f32)
        out_shape, args = (out, out, out, out, out_t, out_t), (inv_l, inv[:, None])
    else:
        out_shape, args = (out, out, out), (inv_l,)
    return pl.pallas_call(
        functools.partial(_rope_table_kernel, period=period, offset=offset, n_mask=n_mask),
        out_shape=out_shape,
        name="rope_tables",
    )(*args)


def _conv_layer_kernel(*refs, bb_n, tt, width, has_state, n_chain):
    if has_state:
        (x_ref, st_ref, gpre_ref, win_ref, wdw_ref, bdw_ref, gln_ref, bln_ref, wout_ref, gpost_ref,
         h_ref, cst_ref, vbuf, ybuf) = refs
    else:
        (x_ref, gpre_ref, win_ref, wdw_ref, bdw_ref, gln_ref, bln_ref, wout_ref, gpost_ref,
         h_ref, cst_ref, vbuf, ybuf) = refs
        st_ref = None
    d = x_ref.shape[-1]
    c = wdw_ref.shape[-1]
    n_cb = c // LANES
    j = pl.program_id(1)
    pad = HIST - (width - 1)
    per = bb_n // n_chain
    rows = per * tt

    @pl.when(j == 0)
    def _():
        for bb in range(bb_n):
            for cb in range(n_cb):
                if has_state:
                    vbuf[bb, cb, 0:SUBLANES, :] = jnp.zeros((SUBLANES, LANES), _f32)
                    vbuf[bb, cb, pad:HIST, :] = st_ref[0, bb, :, cb * LANES:(cb + 1) * LANES]
                else:
                    vbuf[bb, cb, 0:HIST, :] = jnp.zeros((HIST, LANES), _f32)

    xs, zs = [], []
    for ch in range(n_chain):
        x = x_ref[ch * per:(ch + 1) * per].reshape(rows, d)
        u = _rms(x, gpre_ref[...]).astype(_bf16)
        p = jnp.dot(u, win_ref[...], preferred_element_type=_f32)
        v = p[:, :c] * jax.nn.sigmoid(p[:, c:2 * c])
        xs.append(x)
        zs.append(p[:, 2 * c:])
        for b1 in range(per):
            for cb in range(n_cb):
                vbuf[ch * per + b1, cb, HIST:HIST + tt, :] = v[b1 * tt:(b1 + 1) * tt, cb * LANES:(cb + 1) * LANES]

    rc = min(tt, 128)
    nr = rc // SUBLANES
    for cb in range(n_cb):
        lanes = pl.ds(cb * LANES, LANES)
        wb = [jnp.broadcast_to(wdw_ref[k:k + 1, lanes], (SUBLANES, LANES)) for k in range(width)]
        bias = jnp.broadcast_to(bdw_ref[0:1, lanes], (SUBLANES, LANES))
        for bb in range(bb_n):
            def chunk(i, carry, bb=bb, cb=cb, lanes=lanes, wb=wb, bias=bias):
                r0 = pl.multiple_of(i * rc, rc)
                for r in range(nr):
                    acc = bias
                    for k in range(width):
                        acc = acc + wb[k] * vbuf[bb, cb, pl.ds(r0 + SUBLANES * r + pad + k, SUBLANES, stride=1), :]
                    ybuf[pl.ds(bb * tt + r0 + SUBLANES * r, SUBLANES), lanes] = acc
                return carry
            lax.fori_loop(0, tt // rc, chunk, 0)

    for bb in range(bb_n):
        for cb in range(n_cb):
            cst_ref[0, bb, :, cb * LANES:(cb + 1) * LANES] = vbuf[bb, cb, tt + pad:tt + HIST, :]
            vbuf[bb, cb, 0:HIST, :] = vbuf[bb, cb, tt:tt + HIST, :]

    for ch in range(n_chain):
        y = ybuf[ch * rows:(ch + 1) * rows, :]
        mu = jnp.mean(y, axis=-1, keepdims=True)
        yc = y - mu
        var = jnp.mean(yc * yc, axis=-1, keepdims=True)
        yn = yc * lax.rsqrt(var + NORM_EPS) * gln_ref[...] + bln_ref[...]
        m = (jax.nn.silu(yn) * jax.nn.silu(zs[ch])).astype(_bf16)
        out = jnp.dot(m, wout_ref[...], preferred_element_type=_f32)
        h = xs[ch] + _rms(out, gpost_ref[...])
        h_ref[ch * per:(ch + 1) * per] = h.reshape(per, tt, d)


def _conv_layer(x, state, g_pre, w_in, w_dw, b_dw, g_ln, b_ln, w_out, g_post, *, bb_n, tt):
    b, t, d = x.shape
    width, c = w_dw.shape
    has_state = state is not None
    assert t % tt == 0 and b % bb_n == 0 and tt % SUBLANES == 0 and tt >= HIST and c % LANES == 0
    assert width - 1 <= HIST
    grid = (b // bb_n, t // tt)
    n_chain = bb_n if tt >= MIN_CHAIN_ROWS else 1
    in_specs = [pl.BlockSpec((bb_n, tt, d), lambda i, j: (i, j, 0))]
    args = [x]
    if has_state:
        in_specs.append(pl.BlockSpec((1, bb_n, width - 1, c), lambda i, j: (0, i, 0, 0)))
        args.append(state)
    in_specs += [_const_spec((1, d)), _const_spec(w_in.shape), _const_spec(w_dw.shape), _const_spec((1, c)),
                 _const_spec((1, c)), _const_spec((1, c)), _const_spec(w_out.shape), _const_spec((1, d))]
    args += [g_pre[None], w_in, w_dw, b_dw[None], g_ln[None], b_ln[None], w_out, g_post[None]]
    out_shape = (jax.ShapeDtypeStruct((b, t, d), _f32),
                 jax.ShapeDtypeStruct((1, b, width - 1, c), _f32))
    out_specs = (pl.BlockSpec((bb_n, tt, d), lambda i, j: (i, j, 0)),
                 pl.BlockSpec((1, bb_n, width - 1, c), lambda i, j: (0, i, 0, 0)))
    return pl.pallas_call(
        functools.partial(_conv_layer_kernel, bb_n=bb_n, tt=tt, width=width, has_state=has_state,
                          n_chain=n_chain),
        grid=grid, in_specs=in_specs, out_specs=out_specs, out_shape=out_shape,
        scratch_shapes=[pltpu.VMEM((bb_n, c // LANES, HIST + tt, LANES), _f32),
                        pltpu.VMEM((bb_n * tt, c), _f32)],
        compiler_params=pltpu.CompilerParams(dimension_semantics=("arbitrary", "arbitrary"),
                                             vmem_limit_bytes=VMEM_LIMIT),
        name="conv_layer",
    )(*args)


def _rope_block(x, c_t, sa_t, sb_t):
    half = QK_ROPE // 2
    return x * c_t + pltpu.roll(x, LANES - half, 1) * sa_t + pltpu.roll(x, half, 1) * sb_t


def _proj_kernel(*refs, kv_lora, feature_major, n_chain):
    if feature_major:
        (h_ref, ct_ref, sat_ref, sbt_ref, et_ref, cost_ref, sint_ref, gkv_ref, wkva_ref, gkva_ref, gpre_ref,
         wqa_ref, gqa_ref, wqb_ref, wk_ref, wv_ref, c_ref, kr_ref, q_ref, k_ref, v_ref) = refs
    else:
        (h_ref, ct_ref, sat_ref, sbt_ref, gkv_ref, wkva_ref, gkva_ref, gpre_ref,
         wqa_ref, gqa_ref, wqb_ref, c_ref, kr_ref, q_ref) = refs
    rows = h_ref.shape[0] // n_chain
    for ch in range(n_chain):
        rs = slice(ch * rows, (ch + 1) * rows)
        h = h_ref[rs, :]
        hn = h * lax.rsqrt(jnp.mean(h * h, axis=-1, keepdims=True) + NORM_EPS)
        c_t, sa_t, sb_t = ct_ref[rs, :], sat_ref[rs, :], sbt_ref[rs, :]

        lat = jnp.dot((hn * gkv_ref[...]).astype(_bf16), wkva_ref[...], preferred_element_type=_f32)
        cl = _rms(lat[:, :kv_lora], gkva_ref[...])
        c_ref[rs, :] = cl
        kr_blk = _rope_block(lat[:, kv_lora:kv_lora + HEAD_PAD], c_t, sa_t, sb_t)
        kr_ref[rs, :] = kr_blk[:, QK_NOPE:QK_NOPE + QK_ROPE]

        qa = jnp.dot((hn * gpre_ref[...]).astype(_bf16), wqa_ref[...], preferred_element_type=_f32)
        qn = _rms(qa, gqa_ref[...]).astype(_bf16)

        if not feature_major:
            q = jnp.dot(qn, wqb_ref[...], preferred_element_type=_f32) * Q_SCALE
            for hh in range(N_HEADS):
                sl = slice(hh * HEAD_PAD, (hh + 1) * HEAD_PAD)
                q_ref[rs, sl] = _rope_block(q[:, sl], c_t, sa_t, sb_t).astype(_bf16)
            continue

        cb = cl.astype(_bf16)
        k = jnp.dot(cb, wk_ref[...], preferred_element_type=_f32)
        kr_e = kr_blk + et_ref[rs, :]
        for hh in range(N_HEADS):
            sl = slice(hh * HEAD_PAD, (hh + 1) * HEAD_PAD)
            k_ref[rs, sl] = (k[:, sl] + kr_e).astype(_bf16)
        v_ref[0, :, rs] = lax.dot_general(wv_ref[...], cb, _NT, preferred_element_type=_f32).astype(_bf16)

        q_t = lax.dot_general(wqb_ref[...], qn, _NT, preferred_element_type=_f32)
        cos_t, sin_t = cost_ref[:, rs], sint_ref[:, rs]
        half = QK_ROPE // 2
        for hh in range(N_HEADS):
            b0 = hh * HEAD_PAD
            x1 = q_t[b0 + QK_NOPE:b0 + QK_NOPE + half]
            x2 = q_t[b0 + QK_NOPE + half:b0 + MASK_AT]
            q_ref[0, b0:b0 + QK_NOPE, rs] = (q_t[b0:b0 + QK_NOPE] * Q_SCALE).astype(_bf16)
            q_ref[0, b0 + QK_NOPE:b0 + QK_NOPE + half, rs] = (x1 * cos_t - x2 * sin_t).astype(_bf16)
            q_ref[0, b0 + QK_NOPE + half:b0 + MASK_AT, rs] = (x2 * cos_t + x1 * sin_t).astype(_bf16)
            q_ref[0, b0 + MASK_AT:b0 + HEAD_PAD, rs] = jnp.zeros((HEAD_PAD - MASK_AT, rows), _bf16)


def _projections(h2, seq_len, tables, g_kv_in, w_kva, g_kv_a, g_pre, w_qa, g_q_a, w_qb, w_k, w_v, *, tt,
                 feature_major):
    n, d = h2.shape
    kv_lora = g_kv_a.shape[0]
    q_lora = g_q_a.shape[0]
    assert n % tt == 0 and tt % CHAIN_ROWS == 0 and w_qa.shape[1] == q_lora
    nt = max(seq_len // tt, 1)
    tok = lambda w: pl.BlockSpec((tt, w), lambda i: (i, 0))
    tab = pl.BlockSpec((tt, LANES), lambda i: (i % nt, 0))
    fm = lambda w: pl.BlockSpec((1, w, tt), lambda i: (i // nt, 0, i % nt))
    half = QK_ROPE // 2
    in_specs = [tok(d), tab, tab, tab]
    if feature_major:
        tab_t = pl.BlockSpec((half, tt), lambda i: (0, i % nt))
        in_specs += [tab, tab_t, tab_t]
    in_specs += [_const_spec((1, d)), _const_spec(w_kva.shape), _const_spec((1, kv_lora)),
                 _const_spec((1, d)), _const_spec(w_qa.shape), _const_spec((1, q_lora)), _const_spec(w_qb.shape)]
    args = [h2, *tables, g_kv_in[None], w_kva, g_kv_a[None], g_pre[None], w_qa, g_q_a[None], w_qb]
    out_shape = [jax.ShapeDtypeStruct((n, kv_lora), _f32), jax.ShapeDtypeStruct((n, QK_ROPE), _f32)]
    out_specs = [tok(kv_lora), tok(QK_ROPE)]
    if feature_major:
        nb = n // seq_len
        in_specs += [_const_spec(w_k.shape), _const_spec(w_v.shape)]
        args += [w_k, w_v]
        out_shape += [jax.ShapeDtypeStruct((nb, N_HEADS * HEAD_PAD, seq_len), _bf16),
                      jax.ShapeDtypeStruct((n, N_HEADS * HEAD_PAD), _bf16),
                      jax.ShapeDtypeStruct((nb, N_HEADS * V_HEAD, seq_len), _bf16)]
        out_specs += [fm(N_HEADS * HEAD_PAD), tok(N_HEADS * HEAD_PAD), fm(N_HEADS * V_HEAD)]
    else:
        out_shape.append(jax.ShapeDtypeStruct((n, N_HEADS * HEAD_PAD), _bf16))
        out_specs.append(tok(N_HEADS * HEAD_PAD))
    return pl.pallas_call(
        functools.partial(_proj_kernel, kv_lora=kv_lora, feature_major=feature_major,
                          n_chain=tt // CHAIN_ROWS),
        grid=(n // tt,), in_specs=in_specs, out_specs=tuple(out_specs), out_shape=tuple(out_shape),
        compiler_params=pltpu.CompilerParams(dimension_semantics=("arbitrary",), vmem_limit_bytes=VMEM_LIMIT),
        name="projections",
    )(*args)


ATTN_STEPS = 8
ONES_ROWS = 16


def _attn_prompt_kernel(q_ref, k_ref, v_ref, o_ref, s_a, s_b, acc0, acc1, m0, m1, mtab, ones_ref, *, tq):
    t = q_ref.shape[2]
    nq = t // tq
    total = nq * (nq + 1) // 2
    assert total % ATTN_STEPS == 0
    n_mask = tq // CHUNK
    pad_rows = HEAD_PAD - MASK_AT
    r = lax.broadcasted_iota(jnp.int32, (pad_rows, tq), 0)
    qc = _chunk_of(lax.broadcasted_iota(jnp.int32, (pad_rows, tq), 1))
    mtab[0] = jnp.zeros((pad_rows, tq), _bf16)
    mtab[1] = jnp.where((r < n_mask) & (r > qc), MASK_NEG, 0.0).astype(_bf16)
    ones_ref[...] = jnp.ones((ONES_ROWS, tq), _bf16)
    accs, ms = (acc0, acc1), (m0, m1)

    for hh in range(2):
        ms[hh][...] = jnp.full(ms[hh].shape, NEG_INF, _f32)
        accs[hh][...] = jnp.zeros(accs[hh].shape, _f32)

    def nxt(pair):
        i, j = pair
        wrap = j + 1 > i
        return jnp.where(wrap, i + 1, i), jnp.where(wrap, 0, j + 1)

    def prefetch(pair, bank, slot):
        i, j = pair
        i = jnp.minimum(i, nq - 1)
        q0 = pl.multiple_of(i * tq, tq)
        k0 = pl.multiple_of(j * tq, tq)
        msel = mtab[jnp.where(i == j, 1, 0)]
        for hh in range(2):
            q_t = jnp.concatenate([q_ref[0, hh * HEAD_PAD:hh * HEAD_PAD + MASK_AT, pl.ds(q0, tq)], msel], axis=0)
            k_t = k_ref[0, pl.ds(k0, tq), hh * HEAD_PAD:(hh + 1) * HEAD_PAD]
            bank[slot, hh] = jnp.dot(k_t, q_t, preferred_element_type=_f32)

    def consume(pair, bank, slot):
        i, j = pair
        k0 = pl.multiple_of(j * tq, tq)
        for hh in range(2):
            m_old = ms[hh][i]
            m_new = jnp.maximum(m_old, jnp.max(bank[slot, hh], axis=0, keepdims=True))
            alpha = jnp.exp2(m_old - m_new)
            p = jnp.exp2(bank[slot, hh] - m_new).astype(_bf16)
            v_aug = jnp.concatenate([v_ref[0, hh * V_HEAD:(hh + 1) * V_HEAD, pl.ds(k0, tq)], ones_ref[...]], axis=0)
            accs[hh][i] = alpha * accs[hh][i] + jnp.dot(v_aug, p, preferred_element_type=_f32)
            ms[hh][i] = m_new

    def stage(first, src, dst):
        pairs = [first]
        for _ in range(2 * ATTN_STEPS - 1):
            pairs.append(nxt(pairs[-1]))
        for s in range(ATTN_STEPS):
            if dst is not None:
                prefetch(pairs[ATTN_STEPS + s], dst, s)
            consume(pairs[s], src, s)
        return pairs[ATTN_STEPS]

    zero = (jnp.int32(0), jnp.int32(0))
    pair = zero
    for s in range(ATTN_STEPS):
        prefetch(pair, s_a, s)
        pair = nxt(pair)

    def body(u, first):
        return stage(stage(first, s_a, s_b), s_b, s_a)

    n_stages = total // ATTN_STEPS
    first = lax.fori_loop(0, n_stages // 2, body, zero)
    if n_stages % 2:
        stage(first, s_a, s_b)

    def finish(i, carry):
        q0 = pl.multiple_of(i * tq, tq)
        o = [accs[hh][i][:V_HEAD] / accs[hh][i][V_HEAD:V_HEAD + 1] for hh in range(2)]
        o_ref[0, pl.ds(q0, tq), :] = jnp.concatenate(o, axis=0).T.astype(o_ref.dtype)
        return carry

    lax.fori_loop(0, nq, finish, 0, unroll=4)


def _attention_prompt(q_t, k, v_t, *, tq):
    b, _, t = q_t.shape
    assert t % tq == 0 and tq % CHUNK == 0 and tq // CHUNK <= HEAD_PAD - MASK_AT
    n_pairs = N_HEADS // 2
    nq = t // tq
    bank = pltpu.VMEM((ATTN_STEPS, 2, tq, tq), _f32)
    acc = pltpu.VMEM((nq, V_HEAD + ONES_ROWS, tq), _f32)
    mx = pltpu.VMEM((nq, 1, tq), _f32)
    return pl.pallas_call(
        functools.partial(_attn_prompt_kernel, tq=tq),
        grid=(b, n_pairs),
        in_specs=[pl.BlockSpec((1, 2 * HEAD_PAD, t), lambda i, j: (i, j, 0)),
                  pl.BlockSpec((1, t, 2 * HEAD_PAD), lambda i, j: (i, 0, j)),
                  pl.BlockSpec((1, 2 * V_HEAD, t), lambda i, j: (i, j, 0))],
        out_specs=pl.BlockSpec((1, t, 2 * V_HEAD), lambda i, j: (i, 0, j)),
        out_shape=jax.ShapeDtypeStruct((b, t, N_HEADS * V_HEAD), _bf16),
        scratch_shapes=[bank, bank, acc, acc, mx, mx, pltpu.VMEM((2, HEAD_PAD - MASK_AT, tq), _bf16),
                        pltpu.VMEM((ONES_ROWS, tq), _bf16)],
        compiler_params=pltpu.CompilerParams(dimension_semantics=("arbitrary", "arbitrary"),
                                             vmem_limit_bytes=VMEM_LIMIT),
        name="attention_prompt",
    )(q_t, k, v_t)


def _attn_sample_kernel(q_ref, cc_ref, ckr_ref, cn_ref, krn_ref, wuk_ref, wuv_ref, o_ref, *, past_len, mask_new):
    ts = q_ref.shape[1]
    q = q_ref[0]
    q_lat, q_rope = [], []
    for hh in range(N_HEADS):
        qh = q[:, hh * HEAD_PAD:(hh + 1) * HEAD_PAD]
        q_lat.append(jnp.dot(qh[:, :QK_NOPE], wuk_ref[hh], preferred_element_type=_f32).astype(_bf16))
        q_rope.append(qh[:, QK_NOPE:QK_NOPE + QK_ROPE])
    q_lat = jnp.concatenate(q_lat, axis=0)
    q_rope = jnp.concatenate(q_rope, axis=0)
    cc = cc_ref[0].astype(_bf16)
    cn = cn_ref[0].astype(_bf16)
    s_c = (lax.dot_general(q_lat, cc, _NT, preferred_element_type=_f32)
           + lax.dot_general(q_rope, ckr_ref[0].astype(_bf16), _NT, preferred_element_type=_f32))
    s_n = (lax.dot_general(q_lat, cn, _NT, preferred_element_type=_f32)
           + lax.dot_general(q_rope, krn_ref[0].astype(_bf16), _NT, preferred_element_type=_f32))
    if mask_new:
        assert ts & (ts - 1) == 0
        rows = lax.broadcasted_iota(jnp.int32, s_n.shape, 0) & (ts - 1)
        cols = lax.broadcasted_iota(jnp.int32, s_n.shape, 1)
        s_n = jnp.where(_chunk_of(past_len + cols) <= _chunk_of(past_len + rows), s_n, NEG_INF)
    m = jnp.maximum(jnp.max(s_c, axis=1, keepdims=True), jnp.max(s_n, axis=1, keepdims=True))
    p_c = jnp.exp2(s_c - m)
    p_n = jnp.exp2(s_n - m)
    l = jnp.sum(p_c, axis=1, keepdims=True) + jnp.sum(p_n, axis=1, keepdims=True)
    o_lat = (jnp.dot(p_c.astype(_bf16), cc, preferred_element_type=_f32)
             + jnp.dot(p_n.astype(_bf16), cn, preferred_element_type=_f32)) / l
    o_lat = o_lat.astype(_bf16)
    outs = [jnp.dot(o_lat[hh * ts:(hh + 1) * ts], wuv_ref[hh], preferred_element_type=_f32)
            for hh in range(N_HEADS)]
    o_ref[0] = jnp.concatenate(outs, axis=1).astype(o_ref.dtype)


def _attention_sample(q, cache_c, cache_kr, c_new, kr_new, w_uk_t, w_uv):
    b, ts, _ = q.shape
    past = cache_c.shape[1]
    kv_lora = cache_c.shape[2]
    mask_new = (past // CHUNK) != ((past + ts - 1) // CHUNK)
    blk = lambda s, w: pl.BlockSpec((1, s, w), lambda i: (i, 0, 0))
    return pl.pallas_call(
        functools.partial(_attn_sample_kernel, past_len=past, mask_new=mask_new),
        grid=(b,),
        in_specs=[blk(ts, N_HEADS * HEAD_PAD), blk(past, kv_lora), blk(past, QK_ROPE), blk(ts, kv_lora),
                  blk(ts, QK_ROPE), _const_spec(w_uk_t.shape), _const_spec(w_uv.shape)],
        out_specs=blk(ts, N_HEADS * V_HEAD),
        out_shape=jax.ShapeDtypeStruct((b, ts, N_HEADS * V_HEAD), _bf16),
        compiler_params=pltpu.CompilerParams(dimension_semantics=("arbitrary",), vmem_limit_bytes=VMEM_LIMIT),
        name="attention_sample",
    )(q, cache_c, cache_kr, c_new, kr_new, w_uk_t, w_uv)


def _out_kernel(o_ref, h_ref, gpre_ref, wz_ref, w_ref, g_ref, y_ref, *, n_chain):
    rows = h_ref.shape[0] // n_chain
    for ch in range(n_chain):
        rs = slice(ch * rows, (ch + 1) * rows)
        h = h_ref[rs, :]
        z = jnp.dot(_rms(h, gpre_ref[...]).astype(_bf16), wz_ref[...], preferred_element_type=_f32)
        m = (o_ref[rs, :].astype(_f32) * jax.nn.silu(z)).astype(_bf16)
        out = jnp.dot(m, w_ref[...], preferred_element_type=_f32)
        y_ref[rs, :] = h + _rms(out, g_ref[...])


def _out_proj(o2, h2, g_pre, w_z, w_out, g_post, *, tt):
    n, d = h2.shape
    w = o2.shape[1]
    assert n % tt == 0 and tt % CHAIN_ROWS == 0 and w_z.shape == (d, w)
    return pl.pallas_call(
        functools.partial(_out_kernel, n_chain=tt // CHAIN_ROWS),
        grid=(n // tt,),
        in_specs=[pl.BlockSpec((tt, w), lambda i: (i, 0)), pl.BlockSpec((tt, d), lambda i: (i, 0)),
                  _const_spec((1, d)), _const_spec(w_z.shape), _const_spec(w_out.shape), _const_spec((1, d))],
        out_specs=pl.BlockSpec((tt, d), lambda i: (i, 0)),
        out_shape=jax.ShapeDtypeStruct((n, d), _f32),
        compiler_params=pltpu.CompilerParams(dimension_semantics=("arbitrary",), vmem_limit_bytes=VMEM_LIMIT),
        name="out_proj",
    )(o2, h2, g_pre[None], w_z, w_out, g_post[None])


def _prep_weights(w_a_in, w_a_out, w_kv_a, w_kv_b, w_b_in, w_q_b, w_b_out):
    kv_lora = w_kv_b.shape[0]
    pad_tail = HEAD_PAD - MASK_AT
    wkva = jnp.concatenate([w_kv_a[:, :kv_lora], jnp.zeros((w_kv_a.shape[0], QK_NOPE), _f32),
                            w_kv_a[:, kv_lora:], jnp.zeros((w_kv_a.shape[0], pad_tail), _f32)], axis=1)
    q_lora = w_q_b.shape[0]
    wqb = w_q_b.reshape(q_lora, N_HEADS, MASK_AT)
    wqb = jnp.pad(wqb, ((0, 0), (0, 0), (0, pad_tail))).reshape(q_lora, N_HEADS * HEAD_PAD)
    wkvb = w_kv_b.reshape(kv_lora, N_HEADS, QK_NOPE + V_HEAD)
    w_uk, w_uv = wkvb[..., :QK_NOPE], wkvb[..., QK_NOPE:]
    w_k = jnp.pad(w_uk, ((0, 0), (0, 0), (0, HEAD_PAD - QK_NOPE))).reshape(kv_lora, N_HEADS * HEAD_PAD)
    w_v = w_uv.reshape(kv_lora, N_HEADS * V_HEAD)
    bf = lambda a: a.astype(_bf16)
    return dict(w_in=bf(w_a_in), w_out_a=bf(w_a_out), wkva=bf(wkva), w_qa=bf(w_b_in[:, :q_lora]),
                w_z=bf(w_b_in[:, q_lora:]), wqb=bf(wqb),
                wqb_t=bf(wqb.T), w_k=bf(w_k), w_v_t=bf(w_v.T), w_uk_t=bf(jnp.transpose(w_uk, (1, 2, 0))),
                w_uv_h=bf(jnp.transpose(w_uv, (1, 0, 2))), w_out_b=bf(w_b_out))


def _trunk(x, state, past, pw, gains, *, bb_n, tt, tok_tile, tq):
    (g_pre, g_post, w_dw, b_dw, g_ln, b_ln, g_kv_in, g_kv_a, g_q_a) = gains
    b, t, d = x.shape
    prompt = past is None
    offset = 0 if prompt else past[0].shape[1]
    h, conv_state = _conv_layer(x, state, g_pre[0], pw["w_in"], w_dw, b_dw, g_ln, b_ln, pw["w_out_a"], g_post[0],
                                bb_n=bb_n, tt=tt)
    h2 = h.reshape(b * t, d)
    if t >= tok_tile:
        assert t % tok_tile == 0
        tables = _rope_tables(t, t, offset, tq // CHUNK if prompt else 0)
    else:
        assert tok_tile % t == 0 and not prompt
        tables = _rope_tables(tok_tile, t, offset, 0)
    outs = _projections(h2, t, tables, g_kv_in, pw["wkva"], g_kv_a, g_pre[1], pw["w_qa"], g_q_a,
                        pw["wqb_t"] if prompt else pw["wqb"], pw["w_k"], pw["w_v_t"], tt=tok_tile,
                        feature_major=prompt)
    c3 = outs[0].reshape(b, t, -1)
    kr3 = outs[1].reshape(b, t, -1)
    if prompt:
        q_t, k, v_t = outs[2:]
        o = _attention_prompt(q_t, k.reshape(b, t, -1), v_t, tq=tq)
    else:
        o = _attention_sample(outs[2].reshape(b, t, -1), past[0], past[1], c3, kr3, pw["w_uk_t"], pw["w_uv_h"])
    y = _out_proj(o.reshape(b * t, -1), h2, g_pre[1], pw["w_z"], pw["w_out_b"], g_post[1], tt=OUT_TILE)
    return y.reshape(b, t, d), conv_state, c3, kr3


def kernel(x_prompt, x_sample, state_conv, cache_kv_latent, cache_k_rope, g_pre, g_post, w_a_in, w_a_dw, b_a_dw, g_a_ln, b_a_ln, w_a_out, g_kv_in, w_kv_a, g_kv_a, w_kv_b, w_b_in, g_q_a, w_q_b, w_b_out):
    assert w_a_in.shape[0] == 1 and w_b_in.shape[0] == 1, "one conv layer followed by one attention layer"
    pw = _prep_weights(w_a_in[0], w_a_out[0], w_kv_a, w_kv_b, w_b_in[0], w_q_b[0], w_b_out[0])
    gains = (g_pre, g_post, w_a_dw[0], b_a_dw[0], g_a_ln[0], b_a_ln[0], g_kv_in, g_kv_a, g_q_a[0])
    y_p, cs_p, c_p, kr_p = _trunk(x_prompt, None, None, pw, gains, bb_n=2, tt=256, tok_tile=256, tq=256)
    y_s, cs_s, c_s, kr_s = _trunk(x_sample, state_conv, (cache_kv_latent, cache_k_rope), pw, gains,
                                  bb_n=8, tt=x_sample.shape[1], tok_tile=256, tq=256)
    return (y_p, y_s, cs_p, c_p, kr_p, cs_s, c_s, kr_s)
```

```python
import functools
import math

import jax
import jax.numpy as jnp
from jax import lax
from jax.experimental import pallas as pl
from jax.experimental.pallas import tpu as pltpu

CHUNK = 64
N_HEADS = 16
QK_NOPE = 64
QK_ROPE = 32
V_HEAD = 64
ROPE_THETA = 10000.0
NORM_EPS = 1e-6
ATTN_SCALE = 1.0 / math.sqrt(QK_NOPE + QK_ROPE)
LOG2E = 1.4426950408889634
Q_SCALE = ATTN_SCALE * LOG2E

LANES = 128
SUBLANES = 8
HEAD_PAD = 128
MASK_AT = QK_NOPE + QK_ROPE
HIST = 32
MIN_CHAIN_ROWS = 128
CHAIN_ROWS = 256
OUT_TILE = 512
VMEM_LIMIT = 56 * 1024 * 1024
NEG_INF = float("-inf")
MASK_NEG = -1e30

_f32 = jnp.float32
_bf16 = jnp.bfloat16
_NT = (((1,), (1,)), ((), ()))


def _chunk_of(pos):
    assert CHUNK & (CHUNK - 1) == 0
    return lax.shift_right_logical(pos, int(math.log2(CHUNK)))


def _rms(x, g):
    ms = jnp.mean(x * x, axis=-1, keepdims=True)
    return x * lax.rsqrt(ms + NORM_EPS) * g


def _const_spec(shape):
    nd = len(shape)
    return pl.BlockSpec(shape, lambda *_: (0,) * nd, pipeline_mode=pl.Buffered(1))


def _rope_table_kernel(*refs, period, offset, n_mask):
    if n_mask:
        inv_ref, invc_ref, c_ref, sa_ref, sb_ref, e_ref, cost_ref, sint_ref = refs
    else:
        inv_ref, c_ref, sa_ref, sb_ref = refs
    rows = c_ref.shape[0]
    row = lax.broadcasted_iota(jnp.int32, (rows, LANES), 0)
    lane = lax.broadcasted_iota(jnp.int32, (rows, LANES), 1)
    if period < rows:
        assert period & (period - 1) == 0
        row = row & (period - 1)
    ang = (offset + row).astype(_f32) * inv_ref[...]
    cos = jnp.cos(ang)
    sin = jnp.sin(ang)
    half = QK_ROPE // 2
    in_x1 = (lane >= QK_NOPE) & (lane < QK_NOPE + half)
    in_x2 = (lane >= QK_NOPE + half) & (lane < QK_NOPE + QK_ROPE)
    c_ref[...] = jnp.where(lane < QK_NOPE, 1.0, jnp.where(in_x1 | in_x2, cos, 0.0))
    sa_ref[...] = jnp.where(in_x1, -sin, 0.0)
    sb_ref[...] = jnp.where(in_x2, sin, 0.0)
    if n_mask:
        assert period >= rows and n_mask & (n_mask - 1) == 0
        e_ref[...] = jnp.where(lane - MASK_AT == (_chunk_of(offset + row) & (n_mask - 1)), 1.0, 0.0)
        pos_l = (offset + lax.broadcasted_iota(jnp.int32, (half, rows), 1)).astype(_f32)
        ang_t = invc_ref[...] * pos_l
        cost_ref[...] = jnp.cos(ang_t) * Q_SCALE
        sint_ref[...] = jnp.sin(ang_t) * Q_SCALE


def _rope_tables(rows, period, offset, n_mask):
    half = QK_ROPE // 2
    inv = ROPE_THETA ** (-jnp.arange(half, dtype=_f32) / half)
    inv_l = jnp.tile(inv, LANES // half)[None, :]
    out = jax.ShapeDtypeStruct((rows, LANES), _f32)
    if n_mask:
        out_t = jax.ShapeDtypeStruct((half, rows), _f32)
        out_shape, args = (out, out, out, out, out_t, out_t), (inv_l, inv[:, None])
    else:
        out_shape, args = (out, out, out), (inv_l,)
    return pl.pallas_call(
        functools.partial(_rope_table_kernel, period=period, offset=offset, n_mask=n_mask),
        out_shape=out_shape,
        name="rope_tables",
    )(*args)


def _conv_layer_kernel(*refs, bb_n, tt, width, has_state, n_chain):
    if has_state:
        (x_ref, st_ref, gpre_ref, win_ref, wdw_ref, bdw_ref, gln_ref, bln_ref, wout_ref, gpost_ref,
         h_ref, cst_ref, vbuf, ybuf) = refs
    else:
        (x_ref, gpre_ref, win_ref, wdw_ref, bdw_ref, gln_ref, bln_ref, wout_ref, gpost_ref,
         h_ref, cst_ref, vbuf, ybuf) = refs
        st_ref = None
    d = x_ref.shape[-1]
    c = wdw_ref.shape[-1]
    n_cb = c // LANES
    j = pl.program_id(1)
    pad = HIST - (width - 1)
    per = bb_n // n_chain
    rows = per * tt

    @pl.when(j == 0)
    def _():
        for bb in range(bb_n):
            for cb in range(n_cb):
                if has_state:
                    vbuf[bb, cb, 0:SUBLANES, :] = jnp.zeros((SUBLANES, LANES), _f32)
                    vbuf[bb, cb, pad:HIST, :] = st_ref[0, bb, :, cb * LANES:(cb + 1) * LANES]
                else:
                    vbuf[bb, cb, 0:HIST, :] = jnp.zeros((HIST, LANES), _f32)

    xs, zs = [], []
    for ch in range(n_chain):
        x = x_ref[ch * per:(ch + 1) * per].reshape(rows, d)
        u = _rms(x, gpre_ref[...]).astype(_bf16)
        p = jnp.dot(u, win_ref[...], preferred_element_type=_f32)
        v = p[:, :c] * jax.nn.sigmoid(p[:, c:2 * c])
        xs.append(x)
        zs.append(p[:, 2 * c:])
        for b1 in range(per):
            for cb in range(n_cb):
                vbuf[ch * per + b1, cb, HIST:HIST + tt, :] = v[b1 * tt:(b1 + 1) * tt, cb * LANES:(cb + 1) * LANES]

    rc = min(tt, 128)
    nr = rc // SUBLANES
    for cb in range(n_cb):
        lanes = pl.ds(cb * LANES, LANES)
        wb = [jnp.broadcast_to(wdw_ref[k:k + 1, lanes], (SUBLANES, LANES)) for k in range(width)]
        bias = jnp.broadcast_to(bdw_ref[0:1, lanes], (SUBLANES, LANES))
        for bb in range(bb_n):
            def chunk(i, carry, bb=bb, cb=cb, lanes=lanes, wb=wb, bias=bias):
                r0 = pl.multiple_of(i * rc, rc)
                for r in range(nr):
                    acc = bias
                    for k in range(width):
                        acc = acc + wb[k] * vbuf[bb, cb, pl.ds(r0 + SUBLANES * r + pad + k, SUBLANES, stride=1), :]
                    ybuf[pl.ds(bb * tt + r0 + SUBLANES * r, SUBLANES), lanes] = acc
                return carry
            lax.fori_loop(0, tt // rc, chunk, 0)

    for bb in range(bb_n):
        for cb in range(n_cb):
            cst_ref[0, bb, :, cb * LANES:(cb + 1) * LANES] = vbuf[bb, cb, tt + pad:tt + HIST, :]
            vbuf[bb, cb, 0:HIST, :] = vbuf[bb, cb, tt:tt + HIST, :]

    for ch in range(n_chain):
        y = ybuf[ch * rows:(ch + 1) * rows, :]
        mu = jnp.mean(y, axis=-1, keepdims=True)
        yc = y - mu
        var = jnp.mean(yc * yc, axis=-1, keepdims=True)
        yn = yc * lax.rsqrt(var + NORM_EPS) * gln_ref[...] + bln_ref[...]
        m = (jax.nn.silu(yn) * jax.nn.silu(zs[ch])).astype(_bf16)
        out = jnp.dot(m, wout_ref[...], preferred_element_type=_f32)
        h = xs[ch] + _rms(out, gpost_ref[...])
        h_ref[ch * per:(ch + 1) * per] = h.reshape(per, tt, d)


def _conv_layer(x, state, g_pre, w_in, w_dw, b_dw, g_ln, b_ln, w_out, g_post, *, bb_n, tt):
    b, t, d = x.shape
    width, c = w_dw.shape
    has_state = state is not None
    assert t % tt == 0 and b % bb_n == 0 and tt % SUBLANES == 0 and tt >= HIST and c % LANES == 0
    assert width - 1 <= HIST
    grid = (b // bb_n, t // tt)
    n_chain = bb_n if tt >= MIN_CHAIN_ROWS else 1
    in_specs = [pl.BlockSpec((bb_n, tt, d), lambda i, j: (i, j, 0))]
    args = [x]
    if has_state:
        in_specs.append(pl.BlockSpec((1, bb_n, width - 1, c), lambda i, j: (0, i, 0, 0)))
        args.append(state)
    in_specs += [_const_spec((1, d)), _const_spec(w_in.shape), _const_spec(w_dw.shape), _const_spec((1, c)),
                 _const_spec((1, c)), _const_spec((1, c)), _const_spec(w_out.shape), _const_spec((1, d))]
    args += [g_pre[None], w_in, w_dw, b_dw[None], g_ln[None], b_ln[None], w_out, g_post[None]]
    out_shape = (jax.ShapeDtypeStruct((b, t, d), _f32),
                 jax.ShapeDtypeStruct((1, b, width - 1, c), _f32))
    out_specs = (pl.BlockSpec((bb_n, tt, d), lambda i, j: (i, j, 0)),
                 pl.BlockSpec((1, bb_n, width - 1, c), lambda i, j: (0, i, 0, 0)))
    return pl.pallas_call(
        functools.partial(_conv_layer_kernel, bb_n=bb_n, tt=tt, width=width, has_state=has_state,
                          n_chain=n_chain),
        grid=grid, in_specs=in_specs, out_specs=out_specs, out_shape=out_shape,
        scratch_shapes=[pltpu.VMEM((bb_n, c // LANES, HIST + tt, LANES), _f32),
                        pltpu.VMEM((bb_n * tt, c), _f32)],
        compiler_params=pltpu.CompilerParams(dimension_semantics=("arbitrary", "arbitrary"),
                                             vmem_limit_bytes=VMEM_LIMIT),
        name="conv_layer",
    )(*args)


def _rope_block(x, c_t, sa_t, sb_t):
    half = QK_ROPE // 2
    return x * c_t + pltpu.roll(x, LANES - half, 1) * sa_t + pltpu.roll(x, half, 1) * sb_t


def _proj_kernel(*refs, kv_lora, feature_major, n_chain):
    if feature_major:
        (h_ref, ct_ref, sat_ref, sbt_ref, et_ref, cost_ref, sint_ref, gkv_ref, wkva_ref, gkva_ref, gpre_ref,
         wqa_ref, gqa_ref, wqb_ref, wk_ref, wv_ref, c_ref, kr_ref, q_ref, k_ref, v_ref) = refs
    else:
        (h_ref, ct_ref, sat_ref, sbt_ref, gkv_ref, wkva_ref, gkva_ref, gpre_ref,
         wqa_ref, gqa_ref, wqb_ref, c_ref, kr_ref, q_ref) = refs
    rows = h_ref.shape[0] // n_chain
    for ch in range(n_chain):
        rs = slice(ch * rows, (ch + 1) * rows)
        h = h_ref[rs, :]
        hn = h * lax.rsqrt(jnp.mean(h * h, axis=-1, keepdims=True) + NORM_EPS)
        c_t, sa_t, sb_t = ct_ref[rs, :], sat_ref[rs, :], sbt_ref[rs, :]

        lat = jnp.dot((hn * gkv_ref[...]).astype(_bf16), wkva_ref[...], preferred_element_type=_f32)
        cl = _rms(lat[:, :kv_lora], gkva_ref[...])
        c_ref[rs, :] = cl
        kr_blk = _rope_block(lat[:, kv_lora:kv_lora + HEAD_PAD], c_t, sa_t, sb_t)
        kr_ref[rs, :] = kr_blk[:, QK_NOPE:QK_NOPE + QK_ROPE]

        qa = jnp.dot((hn * gpre_ref[...]).astype(_bf16), wqa_ref[...], preferred_element_type=_f32)
        qn = _rms(qa, gqa_ref[...]).astype(_bf16)

        if not feature_major:
            q = jnp.dot(qn, wqb_ref[...], preferred_element_type=_f32) * Q_SCALE
            for hh in range(N_HEADS):
                sl = slice(hh * HEAD_PAD, (hh + 1) * HEAD_PAD)
                q_ref[rs, sl] = _rope_block(q[:, sl], c_t, sa_t, sb_t).astype(_bf16)
            continue

        cb = cl.astype(_bf16)
        k = jnp.dot(cb, wk_ref[...], preferred_element_type=_f32)
        kr_e = kr_blk + et_ref[rs, :]
        for hh in range(N_HEADS):
            sl = slice(hh * HEAD_PAD, (hh + 1) * HEAD_PAD)
            k_ref[rs, sl] = (k[:, sl] + kr_e).astype(_bf16)
        v_ref[0, :, rs] = lax.dot_general(wv_ref[...], cb, _NT, preferred_element_type=_f32).astype(_bf16)

        q_t = lax.dot_general(wqb_ref[...], qn, _NT, preferred_element_type=_f32)
        cos_t, sin_t = cost_ref[:, rs], sint_ref[:, rs]
        half = QK_ROPE // 2
        for hh in range(N_HEADS):
            b0 = hh * MASK_AT
            x1 = q_t[b0 + QK_NOPE:b0 + QK_NOPE + half]
            x2 = q_t[b0 + QK_NOPE + half:b0 + MASK_AT]
            q_ref[0, b0:b0 + QK_NOPE, rs] = (q_t[b0:b0 + QK_NOPE] * Q_SCALE).astype(_bf16)
            q_ref[0, b0 + QK_NOPE:b0 + QK_NOPE + half, rs] = (x1 * cos_t - x2 * sin_t).astype(_bf16)
            q_ref[0, b0 + QK_NOPE + half:b0 + MASK_AT, rs] = (x2 * cos_t + x1 * sin_t).astype(_bf16)


def _projections(h2, seq_len, tables, g_kv_in, w_kva, g_kv_a, g_pre, w_qa, g_q_a, w_qb, w_k, w_v, *, tt,
                 feature_major):
    n, d = h2.shape
    kv_lora = g_kv_a.shape[0]
    q_lora = g_q_a.shape[0]
    assert n % tt == 0 and tt % CHAIN_ROWS == 0 and w_qa.shape[1] == q_lora
    nt = max(seq_len // tt, 1)
    tok = lambda w: pl.BlockSpec((tt, w), lambda i: (i, 0))
    tab = pl.BlockSpec((tt, LANES), lambda i: (i % nt, 0))
    fm = lambda w: pl.BlockSpec((1, w, tt), lambda i: (i // nt, 0, i % nt))
    half = QK_ROPE // 2
    in_specs = [tok(d), tab, tab, tab]
    if feature_major:
        tab_t = pl.BlockSpec((half, tt), lambda i: (0, i % nt))
        in_specs += [tab, tab_t, tab_t]
    in_specs += [_const_spec((1, d)), _const_spec(w_kva.shape), _const_spec((1, kv_lora)),
                 _const_spec((1, d)), _const_spec(w_qa.shape), _const_spec((1, q_lora)), _const_spec(w_qb.shape)]
    args = [h2, *tables, g_kv_in[None], w_kva, g_kv_a[None], g_pre[None], w_qa, g_q_a[None], w_qb]
    out_shape = [jax.ShapeDtypeStruct((n, kv_lora), _f32), jax.ShapeDtypeStruct((n, QK_ROPE), _f32)]
    out_specs = [tok(kv_lora), tok(QK_ROPE)]
    if feature_major:
        nb = n // seq_len
        in_specs += [_const_spec(w_k.shape), _const_spec(w_v.shape)]
        args += [w_k, w_v]
        out_shape += [jax.ShapeDtypeStruct((nb, N_HEADS * MASK_AT, seq_len), _bf16),
                      jax.ShapeDtypeStruct((n, N_HEADS * HEAD_PAD), _bf16),
                      jax.ShapeDtypeStruct((nb, N_HEADS * V_HEAD, seq_len), _bf16)]
        out_specs += [fm(N_HEADS * MASK_AT), tok(N_HEADS * HEAD_PAD), fm(N_HEADS * V_HEAD)]
    else:
        out_shape.append(jax.ShapeDtypeStruct((n, N_HEADS * HEAD_PAD), _bf16))
        out_specs.append(tok(N_HEADS * HEAD_PAD))
    return pl.pallas_call(
        functools.partial(_proj_kernel, kv_lora=kv_lora, feature_major=feature_major,
                          n_chain=tt // CHAIN_ROWS),
        grid=(n // tt,), in_specs=in_specs, out_specs=tuple(out_specs), out_shape=tuple(out_shape),
        compiler_params=pltpu.CompilerParams(dimension_semantics=("arbitrary",), vmem_limit_bytes=VMEM_LIMIT),
        name="projections",
    )(*args)


ATTN_STEPS = 8
ONES_ROWS = 16


def _attn_prompt_kernel(q_ref, k_ref, v_ref, o_ref, s_a, s_b, acc0, acc1, m0, m1, mtab, ones_ref, *, tq):
    t = q_ref.shape[2]
    nq = t // tq
    total = nq * (nq + 1) // 2
    assert total % ATTN_STEPS == 0
    n_mask = tq // CHUNK
    pad_rows = HEAD_PAD - MASK_AT
    r = lax.broadcasted_iota(jnp.int32, (pad_rows, tq), 0)
    qc = _chunk_of(lax.broadcasted_iota(jnp.int32, (pad_rows, tq), 1))
    mtab[0] = jnp.zeros((pad_rows, tq), _bf16)
    mtab[1] = jnp.where((r < n_mask) & (r > qc), MASK_NEG, 0.0).astype(_bf16)
    ones_ref[...] = jnp.ones((ONES_ROWS, tq), _bf16)
    accs, ms = (acc0, acc1), (m0, m1)

    for hh in range(2):
        ms[hh][...] = jnp.full(ms[hh].shape, NEG_INF, _f32)
        accs[hh][...] = jnp.zeros(accs[hh].shape, _f32)

    def nxt(pair):
        i, j = pair
        wrap = j + 1 > i
        return jnp.where(wrap, i + 1, i), jnp.where(wrap, 0, j + 1)

    def prefetch(pair, bank, slot):
        i, j = pair
        i = jnp.minimum(i, nq - 1)
        q0 = pl.multiple_of(i * tq, tq)
        k0 = pl.multiple_of(j * tq, tq)
        msel = mtab[jnp.where(i == j, 1, 0)]
        for hh in range(2):
            q_t = jnp.concatenate([q_ref[0, hh * MASK_AT:(hh + 1) * MASK_AT, pl.ds(q0, tq)], msel], axis=0)
            k_t = k_ref[0, pl.ds(k0, tq), hh * HEAD_PAD:(hh + 1) * HEAD_PAD]
            bank[slot, hh] = jnp.dot(k_t, q_t, preferred_element_type=_f32)

    def consume(pair, bank, slot):
        i, j = pair
        k0 = pl.multiple_of(j * tq, tq)
        for hh in range(2):
            m_old = ms[hh][i]
            m_new = jnp.maximum(m_old, jnp.max(bank[slot, hh], axis=0, keepdims=True))
            alpha = jnp.exp2(m_old - m_new)
            p = jnp.exp2(bank[slot, hh] - m_new).astype(_bf16)
            v_aug = jnp.concatenate([v_ref[0, hh * V_HEAD:(hh + 1) * V_HEAD, pl.ds(k0, tq)], ones_ref[...]], axis=0)
            accs[hh][i] = alpha * accs[hh][i] + jnp.dot(v_aug, p, preferred_element_type=_f32)
            ms[hh][i] = m_new

    def stage(first, src, dst):
        pairs = [first]
        for _ in range(2 * ATTN_STEPS - 1):
            pairs.append(nxt(pairs[-1]))
        for s in range(ATTN_STEPS):
            if dst is not None:
                prefetch(pairs[ATTN_STEPS + s], dst, s)
            consume(pairs[s], src, s)
        return pairs[ATTN_STEPS]

    zero = (jnp.int32(0), jnp.int32(0))
    pair = zero
    for s in range(ATTN_STEPS):
        prefetch(pair, s_a, s)
        pair = nxt(pair)

    def body(u, first):
        return stage(stage(first, s_a, s_b), s_b, s_a)

    n_stages = total // ATTN_STEPS
    first = lax.fori_loop(0, n_stages // 2, body, zero)
    if n_stages % 2:
        stage(first, s_a, s_b)

    def finish(i, carry):
        q0 = pl.multiple_of(i * tq, tq)
        o = [accs[hh][i][:V_HEAD] / accs[hh][i][V_HEAD:V_HEAD + 1] for hh in range(2)]
        o_ref[0, pl.ds(q0, tq), :] = jnp.concatenate(o, axis=0).T.astype(o_ref.dtype)
        return carry

    lax.fori_loop(0, nq, finish, 0, unroll=4)


def _attention_prompt(q_t, k, v_t, *, tq):
    b, _, t = q_t.shape
    assert t % tq == 0 and tq % CHUNK == 0 and tq // CHUNK <= HEAD_PAD - MASK_AT
    n_pairs = N_HEADS // 2
    nq = t // tq
    bank = pltpu.VMEM((ATTN_STEPS, 2, tq, tq), _f32)
    acc = pltpu.VMEM((nq, V_HEAD + ONES_ROWS, tq), _f32)
    mx = pltpu.VMEM((nq, 1, tq), _f32)
    return pl.pallas_call(
        functools.partial(_attn_prompt_kernel, tq=tq),
        grid=(b, n_pairs),
        in_specs=[pl.BlockSpec((1, 2 * MASK_AT, t), lambda i, j: (i, j, 0)),
                  pl.BlockSpec((1, t, 2 * HEAD_PAD), lambda i, j: (i, 0, j)),
                  pl.BlockSpec((1, 2 * V_HEAD, t), lambda i, j: (i, j, 0))],
        out_specs=pl.BlockSpec((1, t, 2 * V_HEAD), lambda i, j: (i, 0, j)),
        out_shape=jax.ShapeDtypeStruct((b, t, N_HEADS * V_HEAD), _bf16),
        scratch_shapes=[bank, bank, acc, acc, mx, mx, pltpu.VMEM((2, HEAD_PAD - MASK_AT, tq), _bf16),
                        pltpu.VMEM((ONES_ROWS, tq), _bf16)],
        compiler_params=pltpu.CompilerParams(dimension_semantics=("arbitrary", "arbitrary"),
                                             vmem_limit_bytes=VMEM_LIMIT),
        name="attention_prompt",
    )(q_t, k, v_t)


def _attn_sample_kernel(q_ref, cc_ref, ckr_ref, cn_ref, krn_ref, wuk_ref, wuv_ref, o_ref, *, past_len, mask_new):
    ts = q_ref.shape[1]
    q = q_ref[0]
    q_lat, q_rope = [], []
    for hh in range(N_HEADS):
        qh = q[:, hh * HEAD_PAD:(hh + 1) * HEAD_PAD]
        q_lat.append(jnp.dot(qh[:, :QK_NOPE], wuk_ref[hh], preferred_element_type=_f32).astype(_bf16))
        q_rope.append(qh[:, QK_NOPE:QK_NOPE + QK_ROPE])
    q_lat = jnp.concatenate(q_lat, axis=0)
    q_rope = jnp.concatenate(q_rope, axis=0)
    cc = cc_ref[0].astype(_bf16)
    cn = cn_ref[0].astype(_bf16)
    s_c = (lax.dot_general(q_lat, cc, _NT, preferred_element_type=_f32)
           + lax.dot_general(q_rope, ckr_ref[0].astype(_bf16), _NT, preferred_element_type=_f32))
    s_n = (lax.dot_general(q_lat, cn, _NT, preferred_element_type=_f32)
           + lax.dot_general(q_rope, krn_ref[0].astype(_bf16), _NT, preferred_element_type=_f32))
    if mask_new:
        assert ts & (ts - 1) == 0
        rows = lax.broadcasted_iota(jnp.int32, s_n.shape, 0) & (ts - 1)
        cols = lax.broadcasted_iota(jnp.int32, s_n.shape, 1)
        s_n = jnp.where(_chunk_of(past_len + cols) <= _chunk_of(past_len + rows), s_n, NEG_INF)
    m = jnp.maximum(jnp.max(s_c, axis=1, keepdims=True), jnp.max(s_n, axis=1, keepdims=True))
    p_c = jnp.exp2(s_c - m)
    p_n = jnp.exp2(s_n - m)
    l = jnp.sum(p_c, axis=1, keepdims=True) + jnp.sum(p_n, axis=1, keepdims=True)
    o_lat = (jnp.dot(p_c.astype(_bf16), cc, preferred_element_type=_f32)
             + jnp.dot(p_n.astype(_bf16), cn, preferred_element_type=_f32)) / l
    o_lat = o_lat.astype(_bf16)
    outs = [jnp.dot(o_lat[hh * ts:(hh + 1) * ts], wuv_ref[hh], preferred_element_type=_f32)
            for hh in range(N_HEADS)]
    o_ref[0] = jnp.concatenate(outs, axis=1).astype(o_ref.dtype)


def _attention_sample(q, cache_c, cache_kr, c_new, kr_new, w_uk_t, w_uv):
    b, ts, _ = q.shape
    past = cache_c.shape[1]
    kv_lora = cache_c.shape[2]
    mask_new = (past // CHUNK) != ((past + ts - 1) // CHUNK)
    blk = lambda s, w: pl.BlockSpec((1, s, w), lambda i: (i, 0, 0))
    return pl.pallas_call(
        functools.partial(_attn_sample_kernel, past_len=past, mask_new=mask_new),
        grid=(b,),
        in_specs=[blk(ts, N_HEADS * HEAD_PAD), blk(past, kv_lora), blk(past, QK_ROPE), blk(ts, kv_lora),
                  blk(ts, QK_ROPE), _const_spec(w_uk_t.shape), _const_spec(w_uv.shape)],
        out_specs=blk(ts, N_HEADS * V_HEAD),
        out_shape=jax.ShapeDtypeStruct((b, ts, N_HEADS * V_HEAD), _bf16),
        compiler_params=pltpu.CompilerParams(dimension_semantics=("arbitrary",), vmem_limit_bytes=VMEM_LIMIT),
        name="attention_sample",
    )(q, cache_c, cache_kr, c_new, kr_new, w_uk_t, w_uv)


def _out_kernel(o_ref, h_ref, gpre_ref, wz_ref, w_ref, g_ref, y_ref, *, n_chain):
    rows = h_ref.shape[0] // n_chain
    for ch in range(n_chain):
        rs = slice(ch * rows, (ch + 1) * rows)
        h = h_ref[rs, :]
        z = jnp.dot(_rms(h, gpre_ref[...]).astype(_bf16), wz_ref[...], preferred_element_type=_f32)
        m = (o_ref[rs, :].astype(_f32) * jax.nn.silu(z)).astype(_bf16)
        out = jnp.dot(m, w_ref[...], preferred_element_type=_f32)
        y_ref[rs, :] = h + _rms(out, g_ref[...])


def _out_proj(o2, h2, g_pre, w_z, w_out, g_post, *, tt):
    n, d = h2.shape
    w = o2.shape[1]
    assert n % tt == 0 and tt % CHAIN_ROWS == 0 and w_z.shape == (d, w)
    return pl.pallas_call(
        functools.partial(_out_kernel, n_chain=tt // CHAIN_ROWS),
        grid=(n // tt,),
        in_specs=[pl.BlockSpec((tt, w), lambda i: (i, 0)), pl.BlockSpec((tt, d), lambda i: (i, 0)),
                  _const_spec((1, d)), _const_spec(w_z.shape), _const_spec(w_out.shape), _const_spec((1, d))],
        out_specs=pl.BlockSpec((tt, d), lambda i: (i, 0)),
        out_shape=jax.ShapeDtypeStruct((n, d), _f32),
        compiler_params=pltpu.CompilerParams(dimension_semantics=("arbitrary",), vmem_limit_bytes=VMEM_LIMIT),
        name="out_proj",
    )(o2, h2, g_pre[None], w_z, w_out, g_post[None])


def _prep_weights(w_a_in, w_a_out, w_kv_a, w_kv_b, w_b_in, w_q_b, w_b_out):
    kv_lora = w_kv_b.shape[0]
    pad_tail = HEAD_PAD - MASK_AT
    wkva = jnp.concatenate([w_kv_a[:, :kv_lora], jnp.zeros((w_kv_a.shape[0], QK_NOPE), _f32),
                            w_kv_a[:, kv_lora:], jnp.zeros((w_kv_a.shape[0], pad_tail), _f32)], axis=1)
    q_lora = w_q_b.shape[0]
    wqb = w_q_b.reshape(q_lora, N_HEADS, MASK_AT)
    wqb = jnp.pad(wqb, ((0, 0), (0, 0), (0, pad_tail))).reshape(q_lora, N_HEADS * HEAD_PAD)
    wkvb = w_kv_b.reshape(kv_lora, N_HEADS, QK_NOPE + V_HEAD)
    w_uk, w_uv = wkvb[..., :QK_NOPE], wkvb[..., QK_NOPE:]
    w_k = jnp.pad(w_uk, ((0, 0), (0, 0), (0, HEAD_PAD - QK_NOPE))).reshape(kv_lora, N_HEADS * HEAD_PAD)
    w_v = w_uv.reshape(kv_lora, N_HEADS * V_HEAD)
    bf = lambda a: a.astype(_bf16)
    return dict(w_in=bf(w_a_in), w_out_a=bf(w_a_out), wkva=bf(wkva), w_qa=bf(w_b_in[:, :q_lora]),
                w_z=bf(w_b_in[:, q_lora:]), wqb=bf(wqb),
                wqb_t=bf(w_q_b.T), w_k=bf(w_k), w_v_t=bf(w_v.T), w_uk_t=bf(jnp.transpose(w_uk, (1, 2, 0))),
                w_uv_h=bf(jnp.transpose(w_uv, (1, 0, 2))), w_out_b=bf(w_b_out))


def _trunk(x, state, past, pw, gains, *, bb_n, tt, tok_tile, tq):
    (g_pre, g_post, w_dw, b_dw, g_ln, b_ln, g_kv_in, g_kv_a, g_q_a) = gains
    b, t, d = x.shape
    prompt = past is None
    offset = 0 if prompt else past[0].shape[1]
    h, conv_state = _conv_layer(x, state, g_pre[0], pw["w_in"], w_dw, b_dw, g_ln, b_ln, pw["w_out_a"], g_post[0],
                                bb_n=bb_n, tt=tt)
    h2 = h.reshape(b * t, d)
    if t >= tok_tile:
        assert t % tok_tile == 0
        tables = _rope_tables(t, t, offset, tq // CHUNK if prompt else 0)
    else:
        assert tok_tile % t == 0 and not prompt
        tables = _rope_tables(tok_tile, t, offset, 0)
    outs = _projections(h2, t, tables, g_kv_in, pw["wkva"], g_kv_a, g_pre[1], pw["w_qa"], g_q_a,
                        pw["wqb_t"] if prompt else pw["wqb"], pw["w_k"], pw["w_v_t"], tt=tok_tile,
                        feature_major=prompt)
    c3 = outs[0].reshape(b, t, -1)
    kr3 = outs[1].reshape(b, t, -1)
    if prompt:
        q_t, k, v_t = outs[2:]
        o = _attention_prompt(q_t, k.reshape(b, t, -1), v_t, tq=tq)
    else:
        o = _attention_sample(outs[2].reshape(b, t, -1), past[0], past[1], c3, kr3, pw["w_uk_t"], pw["w_uv_h"])
    y = _out_proj(o.reshape(b * t, -1), h2, g_pre[1], pw["w_z"], pw["w_out_b"], g_post[1], tt=OUT_TILE)
    return y.reshape(b, t, d), conv_state, c3, kr3


def kernel(x_prompt, x_sample, state_conv, cache_kv_latent, cache_k_rope, g_pre, g_post, w_a_in, w_a_dw, b_a_dw, g_a_ln, b_a_ln, w_a_out, g_kv_in, w_kv_a, g_kv_a, w_kv_b, w_b_in, g_q_a, w_q_b, w_b_out):
    assert w_a_in.shape[0] == 1 and w_b_in.shape[0] == 1, "one conv layer followed by one attention layer"
    pw = _prep_weights(w_a_in[0], w_a_out[0], w_kv_a, w_kv_b, w_b_in[0], w_q_b[0], w_b_out[0])
    gains = (g_pre, g_post, w_a_dw[0], b_a_dw[0], g_a_ln[0], b_a_ln[0], g_kv_in, g_kv_a, g_q_a[0])
    y_p, cs_p, c_p, kr_p = _trunk(x_prompt, None, None, pw, gains, bb_n=2, tt=256, tok_tile=256, tq=256)
    y_s, cs_s, c_s, kr_s = _trunk(x_sample, state_conv, (cache_kv_latent, cache_k_rope), pw, gains,
                                  bb_n=8, tt=x_sample.shape[1], tok_tile=256, tq=256)
    return (y_p, y_s, cs_p, c_p, kr_p, cs_s, c_s, kr_s)
```
